```python
import jax
import jax.numpy as jnp
from jax import lax
import numpy as np

D_MODEL = 1024
BATCH = 2
SEQ = 8192
DEPTH = 1
DEC_BATCH = 128
DEC_SEQ = 1
PAST_LEN = 8192
PAGE_SIZE = 128

HEAD_DIM = 64
H_A = 8
KV_A = 2
G_A = H_A // KV_A
H_B = 8
WINDOW_A = 128
DILATIONS = ((128, 1), (512, 4), (2048, 16))
WINDOW_B = 2048
BLOCK = 128
BAND = 128
ROT_DIM = HEAD_DIM // 4
ROPE_THETA = 500000.0
Q_A = H_A * HEAD_DIM
K_A = KV_A * HEAD_DIM
Q_B = H_B * HEAD_DIM
D_MIX = Q_A + Q_B
D_IN = Q_A + 2 * K_A + 3 * Q_B
SPLITS = (Q_A, Q_A + K_A, Q_A + 2 * K_A, Q_A + 2 * K_A + Q_B, Q_A + 2 * K_A + 2 * Q_B)
PEER_HEADS = 8
PEER_QDIM = 256
N_KEYS = 128
N_EXPERTS = N_KEYS * N_KEYS
PEER_TOPK = 16
PEER_CHUNK = 128
EPS = 1e-6

kernel_name = 'hymba_swa_sink_dilated_peer_step'


def rms_norm(x, g):
    xf = x.astype(jnp.float32)
    r = xf * lax.rsqrt(jnp.mean(xf * xf, axis=-1, keepdims=True) + EPS)
    return (r * g.astype(jnp.float32)).astype(x.dtype)


def rope(x, pos):
    inv = ROPE_THETA ** (-jnp.arange(0, ROT_DIM, 2, dtype=jnp.float32) / ROT_DIM)
    ang = pos.astype(jnp.float32)[:, None] * inv[None, :]
    cos, sin = jnp.cos(ang)[:, None, :], jnp.sin(ang)[:, None, :]
    xr = x[..., :ROT_DIM].astype(jnp.float32)
    x1, x2 = xr[..., :ROT_DIM // 2], xr[..., ROT_DIM // 2:]
    rot = jnp.concatenate([x1 * cos - x2 * sin, x2 * cos + x1 * sin], axis=-1)
    return jnp.concatenate([rot.astype(x.dtype), x[..., ROT_DIM:]], axis=-1)


def project(x, pos, norm_g, w_in):
    B, T, _ = x.shape
    z = rms_norm(x, norm_g) @ w_in
    q_a, k_a, v_a, q_b, k_b, v_b = jnp.split(z, SPLITS, axis=-1)
    heads = lambda t, h: t.reshape(B, T, h, HEAD_DIM)
    q_a = rope(heads(q_a, H_A), pos).reshape(B, T, KV_A, G_A, HEAD_DIM)
    k_a = rope(heads(k_a, KV_A), pos)
    v_a = heads(v_a, KV_A)
    q_b = rope(heads(q_b, H_B), pos)
    k_b = rope(heads(k_b, H_B), pos)
    v_b = heads(v_b, H_B)
    return q_a, k_a, v_a, q_b, k_b, v_b


def banded_attention(q, k, v, sink):
    N, L, Hkv, G, D = q.shape
    nb = L // BLOCK
    qb = q.reshape(N, nb, BLOCK, Hkv, G, D)
    kb = k.reshape(N, nb, BLOCK, Hkv, D)
    vb = v.reshape(N, nb, BLOCK, Hkv, D)
    shift = lambda t: jnp.concatenate([jnp.zeros_like(t[:, :1]), t[:, :-1]], axis=1)
    kk = jnp.concatenate([shift(kb), kb], axis=2)
    vv = jnp.concatenate([shift(vb), vb], axis=2)
    s = jnp.einsum('nbqhgd,nbkhd->nbhgqk', qb, kk).astype(jnp.float32) * (D ** -0.5)
    qi = jnp.arange(BLOCK) + BLOCK
    ki = jnp.arange(2 * BLOCK)
    dist = qi[:, None] - ki[None, :]
    band = (dist >= 0) & (dist <= BAND)
    has_prev = (jnp.arange(nb)[:, None, None] > 0) | (ki[None, None, :] >= BLOCK)
    mask = band[None] & has_prev
    s = jnp.where(mask[None, :, None, None], s, -jnp.inf)
    lse = jax.nn.logsumexp(s, axis=-1)
    if sink is not None:
        lse = jnp.logaddexp(lse, sink.astype(jnp.float32)[:, :, None])
    p = jnp.exp(s - lse[..., None])
    o = jnp.einsum('nbhgqk,nbkhd->nbqhgd', p.astype(vv.dtype), vv)
    lse = lse.transpose(0, 1, 4, 2, 3).reshape(N, L, Hkv, G)
    return o.reshape(N, L, Hkv, G, D), lse


def dilated_branch_prompt(q, k, v, d):
    B, T, H, D = q.shape
    span = d * BLOCK
    Tp = -(-T // span) * span
    def sub(t):
        t = jnp.pad(t, ((0, 0), (0, Tp - T), (0, 0), (0, 0)))
        return t.reshape(B, Tp // d, d, H, D).transpose(0, 2, 1, 3, 4).reshape(B * d, Tp // d, H, D)
    o, lse = banded_attention(sub(q)[:, :, :, None], sub(k), sub(v), None)
    o = o[:, :, :, 0].reshape(B, d, Tp // d, H, D).transpose(0, 2, 1, 3, 4).reshape(B, Tp, H, D)[:, :T]
    lse = lse[..., 0].reshape(B, d, Tp // d, H).transpose(0, 2, 1, 3).reshape(B, Tp, H)[:, :T]
    return o, lse


def window_indices(L, S, w, d):
    n = w // d + 1
    idx = L + jnp.arange(S)[:, None] - d * jnp.arange(n)[None, :]
    return jnp.clip(idx, 0, None), idx >= 0


def gathered_attention(q, kv, idx, valid, sink):
    kv_g = kv[:, idx]
    s = jnp.einsum('nshgd,nskhd->nshgk', q, kv_g[:, :, :, 0]).astype(jnp.float32) * (HEAD_DIM ** -0.5)
    s = jnp.where(valid[None, :, None, None, :], s, -jnp.inf)
    lse = jax.nn.logsumexp(s, axis=-1)
    if sink is not None:
        lse = jnp.logaddexp(lse, sink.astype(jnp.float32))
    p = jnp.exp(s - lse[..., None])
    o = jnp.einsum('nshgk,nskhd->nshgd', p.astype(kv.dtype), kv_g[:, :, :, 1])
    return o, lse


def combine_dilations(branches):
    outs = jnp.stack([o for o, _ in branches])
    w = jax.nn.softmax(jnp.stack([s for _, s in branches]), axis=0)
    return jnp.sum(w[..., None].astype(outs.dtype) * outs, axis=0)


def mix_out(x, o_a, o_b, g_a, g_b, w_out):
    B, T = x.shape[:2]
    o = jnp.concatenate([rms_norm(o_a.reshape(B, T, Q_A), g_a), rms_norm(o_b.reshape(B, T, Q_B), g_b)], axis=-1)
    return x + o @ w_out


def peer_ffn(h, w_q, sub_keys, u, v):
    N = h.shape[0]
    nc = -(-N // PEER_CHUNK)
    hp = jnp.pad(h, ((0, nc * PEER_CHUNK - N), (0, 0))).reshape(nc, PEER_CHUNK, D_MODEL)
    def block(hc):
        q = (hc @ w_q).reshape(PEER_CHUNK, PEER_HEADS, 2, PEER_QDIM // 2)
        sc = jnp.einsum('chpd,hpkd->chpk', q, sub_keys).astype(jnp.float32)
        s1, i1 = lax.top_k(sc[:, :, 0], PEER_TOPK)
        s2, i2 = lax.top_k(sc[:, :, 1], PEER_TOPK)
        cand = (s1[..., :, None] + s2[..., None, :]).reshape(PEER_CHUNK, PEER_HEADS, PEER_TOPK * PEER_TOPK)
        cid = (i1[..., :, None] * N_KEYS + i2[..., None, :]).reshape(PEER_CHUNK, PEER_HEADS, PEER_TOPK * PEER_TOPK)
        top_s, pos = lax.top_k(cand, PEER_TOPK)
        eid = jnp.take_along_axis(cid, pos, axis=-1)
        gate = jax.nn.softmax(top_s, axis=-1)
        act = jax.nn.gelu(jnp.einsum('cd,chkd->chk', hc, u[eid]).astype(jnp.float32))
        return jnp.einsum('chk,chkd->cd', (gate * act).astype(v.dtype), v[eid])
    return lax.map(block, hp).reshape(nc * PEER_CHUNK, D_MODEL)[:N]


def setup_inputs(seed: int = 0) -> dict:
    key = jax.random.key(seed)
    ks = jax.random.split(key, 17)
    nrm = lambda k, shape, scale: scale * jax.random.normal(k, shape, jnp.float32)
    w_a_buf = min(WINDOW_A, PAST_LEN)
    w_b_buf = min(WINDOW_B, PAST_LEN)
    return {
        'x_prompt': nrm(ks[0], (BATCH, SEQ, D_MODEL), 1.0),
        'x_sample': nrm(ks[1], (DEC_BATCH, DEC_SEQ, D_MODEL), 1.0),
        'cache_swa_kv': nrm(ks[2], (DEPTH, DEC_BATCH, w_a_buf, 2, KV_A, HEAD_DIM), 1.0),
        'cache_dil_kv': nrm(ks[3], (DEPTH, DEC_BATCH, w_b_buf, 2, H_B, HEAD_DIM), 1.0),
        'attn_norm': 1.0 + nrm(ks[4], (DEPTH, D_MODEL), 0.02),
        'w_in': nrm(ks[5], (DEPTH, D_MODEL, D_IN), D_MODEL ** -0.5),
        'sinks': nrm(ks[6], (DEPTH, H_A), 0.5),
        'out_norm_a': 1.0 + nrm(ks[7], (DEPTH, Q_A), 0.02),
        'out_norm_b': 1.0 + nrm(ks[8], (DEPTH, Q_B), 0.02),
        'w_out': nrm(ks[9], (DEPTH, D_MIX, D_MODEL), D_MIX ** -0.5),
        'ffn_norm': 1.0 + nrm(ks[10], (DEPTH, D_MODEL), 0.02),
        'peer_w_q': nrm(ks[11], (DEPTH, D_MODEL, PEER_HEADS * PEER_QDIM), D_MODEL ** -0.5),
        'peer_sub_keys': nrm(ks[12], (DEPTH, PEER_HEADS, 2, N_KEYS, PEER_QDIM // 2), (PEER_QDIM // 2) ** -0.5),
        'peer_u': nrm(ks[13], (DEPTH, N_EXPERTS, D_MODEL), D_MODEL ** -0.5),
        'peer_v': nrm(ks[14], (DEPTH, N_EXPERTS, D_MODEL), PEER_HEADS ** -0.5),
        'final_norm': 1.0 + nrm(ks[15], (D_MODEL,), 0.02),
    }


def reference(x_prompt, x_sample, cache_swa_kv, cache_dil_kv, attn_norm, w_in, sinks, out_norm_a, out_norm_b,
              w_out, ffn_norm, peer_w_q, peer_sub_keys, peer_u, peer_v, final_norm):
    T = x_prompt.shape[1]
    S = x_sample.shape[1]
    L_A = cache_swa_kv.shape[2]
    L_B = cache_dil_kv.shape[2]
    pos_p = jnp.arange(T)
    pos_s = PAST_LEN + jnp.arange(S)
    xp, xs = x_prompt, x_sample
    swa_p, dil_p, swa_s, dil_s = [], [], [], []
    for l in range(DEPTH):
        sink_l = sinks[l].reshape(KV_A, G_A)
        q_a, k_a, v_a, q_b, k_b, v_b = project(xp, pos_p, attn_norm[l], w_in[l])
        o_a, _ = banded_attention(q_a, k_a, v_a, sink_l)
        o_b = combine_dilations([dilated_branch_prompt(q_b, k_b, v_b, d) for _, d in DILATIONS])
        xp = mix_out(xp, o_a, o_b, out_norm_a[l], out_norm_b[l], w_out[l])
        xp = xp + peer_ffn(rms_norm(xp, ffn_norm[l]).reshape(-1, D_MODEL), peer_w_q[l], peer_sub_keys[l],
                           peer_u[l], peer_v[l]).reshape(xp.shape)
        swa_p.append(jnp.stack([k_a, v_a], axis=2)[:, T - min(WINDOW_A, T):])
        dil_p.append(jnp.stack([k_b, v_b], axis=2)[:, T - min(WINDOW_B, T):])
        q_a, k_a, v_a, q_b, k_b, v_b = project(xs, pos_s, attn_norm[l], w_in[l])
        kv_a = jnp.concatenate([cache_swa_kv[l], jnp.stack([k_a, v_a], axis=2)], axis=1)
        idx_a, val_a = window_indices(L_A, S, WINDOW_A, 1)
        o_a, _ = gathered_attention(q_a, kv_a, idx_a, val_a, sink_l)
        kv_b = jnp.concatenate([cache_dil_kv[l], jnp.stack([k_b, v_b], axis=2)], axis=1)
        branches = []
        for w, d in DILATIONS:
            idx_b, val_b = window_indices(L_B, S, w, d)
            o, lse = gathered_attention(q_b[:, :, :, None], kv_b, idx_b, val_b, None)
            branches.append((o[:, :, :, 0], lse[..., 0]))
        o_b = combine_dilations(branches)
        xs = mix_out(xs, o_a, o_b, out_norm_a[l], out_norm_b[l], w_out[l])
        xs = xs + peer_ffn(rms_norm(xs, ffn_norm[l]).reshape(-1, D_MODEL), peer_w_q[l], peer_sub_keys[l],
                           peer_u[l], peer_v[l]).reshape(xs.shape)
        swa_s.append(kv_a[:, S:])
        dil_s.append(kv_b[:, S:])
    y_prompt = rms_norm(xp, final_norm)
    y_sample = rms_norm(xs, final_norm)
    return (y_prompt, y_sample, jnp.stack(swa_p), jnp.stack(dil_p), jnp.stack(swa_s), jnp.stack(dil_s))
```

```python
import functools

import jax
import jax.numpy as jnp
from jax import lax
from jax.experimental import pallas as pl
from jax.experimental.pallas import tpu as pltpu

F32 = jnp.float32
BF16 = jnp.bfloat16

D_MODEL = 1024
HEAD_DIM = 64
PAST_LEN = 8192
H_A = 8
KV_A = 2
H_B = 8
Q_A = H_A * HEAD_DIM
K_A = KV_A * HEAD_DIM
Q_B = H_B * HEAD_DIM
D_IN = Q_A + 2 * K_A + 3 * Q_B
ROT_DIM = HEAD_DIM // 4
ROPE_THETA = 500000.0
BLOCK = 128
DILATIONS = (1, 4, 16)
CHUNK = BLOCK * DILATIONS[-1]
EPS = 1e-6
LANES = 128
SUBLANES = 8
NEG = -1e30

VMEM_LIMIT = 56 * 1024 * 1024


def _rms(x, g):
    return (x * lax.rsqrt(jnp.mean(x * x, axis=-1, keepdims=True) + EPS)) * g


def _rope_tables(pos):
    inv = ROPE_THETA ** (-jnp.arange(0, ROT_DIM, 2, dtype=F32) / ROT_DIM)
    ang = pos.astype(F32)[:, None] * inv[None, :]
    cos, sin = jnp.cos(ang), jnp.sin(ang)
    t = pos.shape[0]
    half = ROT_DIM // 2
    rest = HEAD_DIM - ROT_DIM
    c = jnp.concatenate([cos, cos, jnp.ones((t, rest), F32)], axis=-1)
    s1 = jnp.concatenate([-sin, jnp.zeros((t, half + rest), F32)], axis=-1)
    s2 = jnp.concatenate([jnp.zeros((t, half), F32), sin, jnp.zeros((t, rest), F32)], axis=-1)
    rep = LANES // HEAD_DIM
    return jnp.tile(c, (1, rep)), jnp.tile(s1, (1, rep)), jnp.tile(s2, (1, rep))


def _proj_kernel(x_ref, g_ref, w_ref, c_ref, s1_ref, s2_ref, qa_ref, kva_ref, qb_ref, kvb_ref):
    h = _rms(x_ref[...], g_ref[...])
    z = jnp.dot(h.astype(BF16), w_ref[...], preferred_element_type=F32)
    c, s1, s2 = c_ref[...], s1_ref[...], s2_ref[...]
    half = ROT_DIM // 2

    def chunk(i, rot):
        zc = z[:, i * LANES:(i + 1) * LANES]
        if not rot:
            return zc
        return zc * c + pltpu.roll(zc, LANES - half, 1) * s1 + pltpu.roll(zc, half, 1) * s2

    nq = Q_A // LANES
    for i in range(nq):
        qa_ref[:, i * LANES:(i + 1) * LANES] = chunk(i, True)
    kva_ref[:, 0:LANES] = chunk(nq, True)
    kva_ref[:, LANES:2 * LANES] = chunk(nq + 1, False)
    base = nq + 2
    for i in range(nq):
        qb_ref[:, i * LANES:(i + 1) * LANES] = chunk(base + i, True)
    for i in range(nq):
        kvb_ref[:, i * LANES:(i + 1) * LANES] = chunk(base + nq + i, True)
    for i in range(nq):
        kvb_ref[:, Q_B + i * LANES:Q_B + (i + 1) * LANES] = chunk(base + 2 * nq + i, False)


def _project(x, pos, g, w_bf16, tm):
    n = x.shape[0]
    c, s1, s2 = _rope_tables(pos)
    row = lambda width: pl.BlockSpec((tm, width), lambda i: (i, 0))
    full = lambda a: pl.BlockSpec(a.shape, lambda i: (0, 0))
    return pl.pallas_call(
        _proj_kernel,
        grid=(n // tm,),
        in_specs=[row(D_MODEL), full(g), full(w_bf16), row(LANES), row(LANES), row(LANES)],
        out_specs=[row(Q_A), row(2 * K_A), row(Q_B), row(2 * Q_B)],
        out_shape=[jax.ShapeDtypeStruct((n, Q_A), F32), jax.ShapeDtypeStruct((n, 2 * K_A), F32),
                   jax.ShapeDtypeStruct((n, Q_B), F32), jax.ShapeDtypeStruct((n, 2 * Q_B), F32)],
        compiler_params=pltpu.CompilerParams(dimension_semantics=("parallel",), vmem_limit_bytes=VMEM_LIMIT),
        name="in_proj",
    )(x, g, w_bf16, c, s1, s2)


def _band_mask(has_prev):
    r = lax.broadcasted_iota(jnp.int32, (BLOCK, 2 * BLOCK), 0)
    k = lax.broadcasted_iota(jnp.int32, (BLOCK, 2 * BLOCK), 1)
    dist = r + BLOCK - k
    band = (dist >= 0) & (dist <= BLOCK)
    return band & ((k >= BLOCK) | has_prev)


def _half_softmax_pv(qm, k, vm, mask):
    s = lax.dot_general(qm.astype(BF16), k.astype(BF16), (((1,), (1,)), ((), ())),
                        preferred_element_type=F32) * (HEAD_DIM ** -0.5)
    s = jnp.where(mask, s, NEG)
    m = jnp.max(s, axis=-1, keepdims=True)
    p = jnp.exp(s - m)
    l = jnp.sum(p, axis=-1, keepdims=True)
    o = jnp.dot(p.astype(BF16), vm.astype(BF16), preferred_element_type=F32)
    return o, m, l


def _attn_prompt_kernel(qa_ref, kva_c_ref, kva_p_ref, qb_ref, kb_c_ref, kb_p_ref, vb_c_ref, vb_p_ref, sink_ref,
                        oa_ref, ob_ref, kva_buf, kb_buf, vb_buf, o_buf, lse_buf):
    hp = pl.program_id(1)
    ck = pl.program_id(2)
    kva_buf[0:CHUNK] = kva_p_ref[0]
    kva_buf[CHUNK:2 * CHUNK] = kva_c_ref[0]
    kb_buf[0:CHUNK] = kb_p_ref[0]
    kb_buf[CHUNK:2 * CHUNK] = kb_c_ref[0]
    vb_buf[0:CHUNK] = vb_p_ref[0]
    vb_buf[CHUNK:2 * CHUNK] = vb_c_ref[0]

    lane = lax.broadcasted_iota(jnp.int32, (1, LANES), 1)
    in_half = [lane < HEAD_DIM, lane >= HEAD_DIM]
    kvh = hp // (H_A // KV_A // 2)
    kv_half = (lane >= kvh * HEAD_DIM) & (lane < (kvh + 1) * HEAD_DIM)
    sink = sink_ref[...]
    nblk = CHUNK // BLOCK

    def body(i, carry):
        start = pl.multiple_of(i * BLOCK, BLOCK)
        has_prev = (ck > 0) | (i > 0)
        mask = _band_mask(has_prev)
        q = qa_ref[0, pl.ds(start, BLOCK), :]
        kv = kva_buf[pl.ds(CHUNK - BLOCK + start, 2 * BLOCK), :]
        k = kv[:, 0:LANES]
        v = jnp.where(kv_half, kv[:, LANES:2 * LANES], 0.0)
        q_sw = pltpu.roll(q, HEAD_DIM, 1)
        acc = jnp.zeros((BLOCK, LANES), F32)
        for a in range(2):
            q_al = jnp.where(kvh == a, q, q_sw)
            qm = jnp.where(kv_half, q_al, 0.0)
            o, m, l = _half_softmax_pv(qm, k, v, mask)
            o = jnp.where(kvh == a, o, pltpu.roll(o, HEAD_DIM, 1))
            acc = acc + jnp.where(in_half[a], o / (l + jnp.exp(sink - m)), 0.0)
        oa_ref[0, pl.ds(start, BLOCK), :] = acc

        for bi, d in enumerate(DILATIONS):
            j, r = i // d, i % d
            qstart = j * (BLOCK * d) + r
            has_prev_b = (ck > 0) | (j > 0)
            mask_b = _band_mask(has_prev_b)
            if d == 1:
                qd = qb_ref[0, pl.ds(pl.multiple_of(qstart, BLOCK), BLOCK), :]
                kd = kb_buf[pl.ds(pl.multiple_of(CHUNK - BLOCK + qstart, BLOCK), 2 * BLOCK), :]
                vd = vb_buf[pl.ds(pl.multiple_of(CHUNK - BLOCK + qstart, BLOCK), 2 * BLOCK), :]
            else:
                qd = qb_ref[0, pl.ds(qstart, BLOCK, stride=d), :]
                kd = kb_buf[pl.ds(CHUNK - BLOCK * d + qstart, 2 * BLOCK, stride=d), :]
                vd = vb_buf[pl.ds(CHUNK - BLOCK * d + qstart, 2 * BLOCK, stride=d), :]
            o_acc = jnp.zeros((BLOCK, LANES), F32)
            lse_acc = jnp.zeros((BLOCK, LANES), F32)
            for a in range(2):
                qm = jnp.where(in_half[a], qd, 0.0)
                vm = jnp.where(in_half[a], vd, 0.0)
                o, m, l = _half_softmax_pv(qm, kd, vm, mask_b)
                o_acc = o_acc + o / l
                lse_acc = lse_acc + jnp.where(in_half[a], m + jnp.log(l), 0.0)
            if d == 1:
                o_buf[bi, pl.ds(pl.multiple_of(qstart, BLOCK), BLOCK), :] = o_acc
                lse_buf[bi, pl.ds(pl.multiple_of(qstart, BLOCK), BLOCK), :] = lse_acc
            else:
                o_buf[bi, pl.ds(qstart, BLOCK, stride=d), :] = o_acc
                lse_buf[bi, pl.ds(qstart, BLOCK, stride=d), :] = lse_acc
        return carry

    lax.fori_loop(0, nblk, body, 0)

    l0, l1, l2 = lse_buf[0], lse_buf[1], lse_buf[2]
    mx = jnp.maximum(jnp.maximum(l0, l1), l2)
    w0, w1, w2 = jnp.exp(l0 - mx), jnp.exp(l1 - mx), jnp.exp(l2 - mx)
    tot = w0 + w1 + w2
    ob_ref[0] = (w0 / tot) * o_buf[0] + (w1 / tot) * o_buf[1] + (w2 / tot) * o_buf[2]


def _attn_prompt(qa, kva, qb, kvb, sink_row, batch, seq):
    qa = qa.reshape(batch, seq, Q_A)
    kva = kva.reshape(batch, seq, 2 * K_A)
    qb = qb.reshape(batch, seq, Q_B)
    kvb = kvb.reshape(batch, seq, 2 * Q_B)
    npair = Q_A // LANES
    cur = lambda b, h, c: (b, c, h)
    prev = lambda b, h, c: (b, jnp.maximum(c - 1, 0), h)
    blk = lambda width, fn: pl.BlockSpec((1, CHUNK, width), fn)
    oa, ob = pl.pallas_call(
        _attn_prompt_kernel,
        grid=(batch, npair, seq // CHUNK),
        in_specs=[
            blk(LANES, cur),
            blk(2 * K_A, lambda b, h, c: (b, c, 0)),
            blk(2 * K_A, lambda b, h, c: (b, jnp.maximum(c - 1, 0), 0)),
            blk(LANES, cur),
            blk(LANES, cur),
            blk(LANES, prev),
            blk(LANES, lambda b, h, c: (b, c, npair + h)),
            blk(LANES, lambda b, h, c: (b, jnp.maximum(c - 1, 0), npair + h)),
            pl.BlockSpec((1, LANES), lambda b, h, c: (0, h)),
        ],
        out_specs=[blk(LANES, cur), blk(LANES, cur)],
        out_shape=[jax.ShapeDtypeStruct((batch, seq, Q_A), F32), jax.ShapeDtypeStruct((batch, seq, Q_B), F32)],
        scratch_shapes=[
            pltpu.VMEM((2 * CHUNK, 2 * K_A), F32),
            pltpu.VMEM((2 * CHUNK, LANES), F32),
            pltpu.VMEM((2 * CHUNK, LANES), F32),
            pltpu.VMEM((len(DILATIONS), CHUNK, LANES), F32),
            pltpu.VMEM((len(DILATIONS), CHUNK, LANES), F32),
        ],
        compiler_params=pltpu.CompilerParams(
            dimension_semantics=("parallel", "parallel", "arbitrary"), vmem_limit_bytes=VMEM_LIMIT),
        name="attn_prompt",
    )(qa, kva, kva, qb, kvb, kvb, kvb, kvb, sink_row)
    return oa.reshape(batch * seq, Q_A), ob.reshape(batch * seq, Q_B)


def _dot_hilo(x, mat):
    hi = x.astype(BF16)
    lo = (x - hi.astype(F32)).astype(BF16)
    return jnp.dot(hi, mat, preferred_element_type=F32) + jnp.dot(lo, mat, preferred_element_type=F32)


def _column_softmax_pv(kq, v, weight, seg, seg_t, extra):
    s = _dot_hilo(kq, seg) * (HEAD_DIM ** -0.5)
    s = jnp.where(weight > 0, s, NEG)
    m = jnp.max(s, axis=0, keepdims=True)
    p = jnp.exp(s - m) * weight
    l = jnp.sum(p, axis=0, keepdims=True)
    if extra is not None:
        l = l + jnp.exp(extra - m)
    p = p / l
    return jnp.sum(_dot_hilo(p, seg_t) * v, axis=0, keepdims=True)


def _attn_sample_kernel(qa_ref, kva_ref, qb_ref, kvb_ref, swa_ref, c16_ref, c4_ref, c1_ref, sink_ref, seg_ref,
                        segt_ref, oa_ref, ob_ref, swa_out_ref):
    nb = qa_ref.shape[0]
    win = swa_ref.shape[1]
    seg, seg_t = seg_ref[...], segt_ref[...]
    lane = lax.broadcasted_iota(jnp.int32, (1, LANES), 1)
    low = lane < HEAD_DIM
    pad = 7

    def expand_kv(x):
        sw = pltpu.roll(x, HEAD_DIM, 1)
        h0 = jnp.where(low, x, sw)
        h1 = jnp.where(low, sw, x)
        return jnp.concatenate([h0, h0, h1, h1], axis=-1)

    def body(n, carry):
        cache = swa_ref[n]
        new = kva_ref[pl.ds(n, 1), :]
        rows = jnp.concatenate([cache, new, jnp.zeros((pad, 2 * K_A), F32)], axis=0)
        rid = lax.broadcasted_iota(jnp.int32, (win + 1 + pad, 1), 0)
        weight = jnp.where(rid <= win, 1.0, 0.0)
        kx = expand_kv(rows[:, 0:K_A])
        vx = expand_kv(rows[:, K_A:2 * K_A])
        oa_ref[pl.ds(n, 1), :] = _column_softmax_pv(kx * qa_ref[pl.ds(n, 1), :], vx, weight, seg, seg_t,
                                                    sink_ref[...])
        swa_out_ref[n, 0:win - 1, :] = cache[1:win, :]
        swa_out_ref[n, win - 1:win, :] = new

        nd = len(DILATIONS)
        new_b = kvb_ref[pl.ds(n, 1), :]
        rows_b = jnp.concatenate([c16_ref[n], c4_ref[n], c1_ref[n], new_b, jnp.zeros((pad, 2 * Q_B), F32)], axis=0)
        rid_b = lax.broadcasted_iota(jnp.int32, (nd * BLOCK + 1 + pad, 1), 0)
        weight_b = jnp.where(rid_b < nd * BLOCK, 1.0, jnp.where(rid_b == nd * BLOCK, float(nd), 0.0))
        ob_ref[pl.ds(n, 1), :] = _column_softmax_pv(rows_b[:, 0:Q_B] * qb_ref[pl.ds(n, 1), :], rows_b[:, Q_B:2 * Q_B],
                                                    weight_b, seg, seg_t, None)
        return carry

    lax.fori_loop(0, nb, body, 0)


def _attn_sample(qa, kva, qb, kvb, cache_swa, cache_dil, sinks, nb):
    n, win, _ = cache_swa.shape
    span = cache_dil.shape[1]
    assert win == BLOCK and span == CHUNK
    seg = (jnp.arange(Q_A)[:, None] // HEAD_DIM == jnp.arange(LANES)[None, :]).astype(BF16)
    sink8 = jnp.concatenate([sinks.astype(F32), jnp.zeros((LANES - H_A,), F32)])[None]
    views = [cache_dil.reshape(n, span // d, d * 2 * Q_B) for d in DILATIONS]
    row = lambda width: pl.BlockSpec((nb, width), lambda i: (i, 0))
    full = lambda a: pl.BlockSpec(a.shape, lambda i: (0, 0))
    last = lambda d: pl.BlockSpec((nb, BLOCK, 2 * Q_B), lambda i: (i, span // d // BLOCK - 1, 0))
    return pl.pallas_call(
        _attn_sample_kernel,
        grid=(n // nb,),
        in_specs=[row(Q_A), row(2 * K_A), row(Q_B), row(2 * Q_B),
                  pl.BlockSpec((nb, win, 2 * K_A), lambda i: (i, 0, 0)),
                  last(16), last(4), last(1), full(sink8), full(seg), full(seg.T)],
        out_specs=[row(Q_A), row(Q_B), pl.BlockSpec((nb, win, 2 * K_A), lambda i: (i, 0, 0))],
        out_shape=[jax.ShapeDtypeStruct((n, Q_A), F32), jax.ShapeDtypeStruct((n, Q_B), F32),
                   jax.ShapeDtypeStruct((n, win, 2 * K_A), F32)],
        compiler_params=pltpu.CompilerParams(dimension_semantics=("parallel",), vmem_limit_bytes=VMEM_LIMIT),
        name="attn_sample",
    )(qa, kva, qb, kvb, cache_swa, views[2], views[1], views[0], sink8, seg, seg.T)


def _shift_cache_kernel(cur_ref, nxt_ref, new_ref, out_ref):
    rows = cur_ref.shape[1]
    out_ref[0, 0:rows - 1, :] = cur_ref[0, 1:rows, :]
    last = pl.program_id(1) == pl.num_programs(1) - 1
    out_ref[0, rows - 1:rows, :] = jnp.where(last, new_ref[0], nxt_ref[0, 0:1, :])


def _shift_cache(cache, new_rows, rows):
    n, span, width = cache.shape
    nblk = span // rows
    per = rows // SUBLANES
    return pl.pallas_call(
        _shift_cache_kernel,
        grid=(n, nblk),
        in_specs=[pl.BlockSpec((1, rows, width), lambda i, j: (i, j, 0)),
                  pl.BlockSpec((1, SUBLANES, width), lambda i, j: (i, jnp.minimum(j + 1, nblk - 1) * per, 0)),
                  pl.BlockSpec((1, 1, width), lambda i, j: (i, 0, 0))],
        out_specs=pl.BlockSpec((1, rows, width), lambda i, j: (i, j, 0)),
        out_shape=jax.ShapeDtypeStruct(cache.shape, cache.dtype),
        compiler_params=pltpu.CompilerParams(dimension_semantics=("parallel", "parallel"),
                                             vmem_limit_bytes=VMEM_LIMIT),
        name="shift_cache",
    )(cache, cache, new_rows)


N_KEYS = 128
PEER_HEADS = 8
PEER_TOPK = 16
PEER_QDIM = 256
N_EXPERTS = N_KEYS * N_KEYS


def _mix_kernel(oa_ref, ob_ref, x_ref, ga_ref, gb_ref, wo_ref, gf_ref, wq_ref, keys_ref, xmid_ref, ht_ref, st_ref):
    o = jnp.concatenate([_rms(oa_ref[...], ga_ref[...]), _rms(ob_ref[...], gb_ref[...])], axis=-1)
    xm = x_ref[...] + jnp.dot(o.astype(BF16), wo_ref[...], preferred_element_type=F32)
    xmid_ref[...] = xm
    h = _rms(xm, gf_ref[...])
    hb = h.astype(BF16)
    ht_ref[...] = h.T.astype(BF16)
    q = jnp.dot(hb, wq_ref[...], preferred_element_type=F32).astype(BF16)
    half = PEER_QDIM // 2
    for i in range(2 * PEER_HEADS):
        st_ref[i] = lax.dot_general(keys_ref[i], q[:, i * half:(i + 1) * half], (((1,), (1,)), ((), ())),
                                    preferred_element_type=F32)


def _mix(oa, ob, x, ga, gb, wo, gf, wq, keys, tm):
    n = x.shape[0]
    nk = keys.shape[0]
    row = lambda width: pl.BlockSpec((tm, width), lambda i: (i, 0))
    full = lambda a: pl.BlockSpec(a.shape, lambda i: (0,) * a.ndim)
    return pl.pallas_call(
        _mix_kernel,
        grid=(n // tm,),
        in_specs=[row(Q_A), row(Q_B), row(D_MODEL), full(ga), full(gb), full(wo), full(gf), full(wq), full(keys)],
        out_specs=[row(D_MODEL), pl.BlockSpec((D_MODEL, tm), lambda i: (0, i)),
                   pl.BlockSpec((nk, N_KEYS, tm), lambda i: (0, 0, i))],
        out_shape=[jax.ShapeDtypeStruct((n, D_MODEL), F32), jax.ShapeDtypeStruct((D_MODEL, n), BF16),
                   jax.ShapeDtypeStruct((nk, N_KEYS, n), F32)],
        compiler_params=pltpu.CompilerParams(dimension_semantics=("parallel",), vmem_limit_bytes=VMEM_LIMIT),
        name="mix_out",
    )(oa, ob, x, ga, gb, wo, gf, wq, keys)


_CAND = [(i, j) for i in range(PEER_TOPK) for j in range(PEER_TOPK) if (i + 1) * (j + 1) <= PEER_TOPK]


def _topk_kernel(s_ref, g1_ref, g2_ref, thr_ref, a_scr, b_scr):
    def per_head(h, carry):
        for p, (g_ref, scr) in enumerate(((g1_ref, a_scr), (g2_ref, b_scr))):
            s = s_ref[2 * h + p]
            e = jnp.exp(s - jnp.max(s, axis=0, keepdims=True))
            g_ref[h] = e
            cur = e
            for k in range(PEER_TOPK):
                mk = jnp.max(cur, axis=0, keepdims=True)
                scr[k, pl.ds(h, 1), :] = jnp.maximum(mk, 0.0)
                cur = jnp.where(cur == mk, -1.0, cur)
        return carry

    lax.fori_loop(0, PEER_HEADS, per_head, 0)

    a = [a_scr[k] for k in range(PEER_TOPK)]
    b = [b_scr[k] for k in range(PEER_TOPK)]
    cand = [a[i] * b[j] for i, j in _CAND]
    cur = list(cand)
    z = jnp.zeros_like(a[0])
    theta = z
    for k in range(PEER_TOPK):
        mk = functools.reduce(jnp.maximum, cur)
        z = z + jnp.maximum(mk, 0.0)
        theta = jnp.where(mk > 0.0, mk, theta)
        cur = [jnp.where(c == mk, -1.0, c) for c in cur]
    rz = 1.0 / z
    thr = jnp.full_like(z, 2.0)
    for (i, j), c in zip(_CAND, cand):
        thr = jnp.minimum(thr, jnp.where(c >= theta, (a[i] * rz) * b[j], 2.0))
    thr_ref[...] = thr
    for h in range(PEER_HEADS):
        g1_ref[h] = g1_ref[h] * rz[h:h + 1, :]


def _peer_topk(st, tb):
    nk, _, n = st.shape
    heads = nk // 2
    g_spec = pl.BlockSpec((heads, N_KEYS, tb), lambda i: (0, 0, i))
    g_shape = jax.ShapeDtypeStruct((heads, N_KEYS, n), F32)
    return pl.pallas_call(
        _topk_kernel,
        grid=(n // tb,),
        in_specs=[pl.BlockSpec((nk, N_KEYS, tb), lambda i: (0, 0, i))],
        out_specs=[g_spec, g_spec, pl.BlockSpec((heads, tb), lambda i: (0, i))],
        out_shape=[g_shape, g_shape, jax.ShapeDtypeStruct((heads, n), F32)],
        scratch_shapes=[pltpu.VMEM((PEER_TOPK, heads, tb), F32), pltpu.VMEM((PEER_TOPK, heads, tb), F32)],
        compiler_params=pltpu.CompilerParams(dimension_semantics=("parallel",), vmem_limit_bytes=VMEM_LIMIT),
        name="peer_topk",
    )(st)


def _transpose_cast_kernel(x_ref, o_ref):
    o_ref[...] = x_ref[...].T.astype(o_ref.dtype)


def _transpose_cast(x, dtype, blk):
    r, c = x.shape
    return pl.pallas_call(
        _transpose_cast_kernel,
        grid=(r // blk, c // blk),
        in_specs=[pl.BlockSpec((blk, blk), lambda i, j: (i, j))],
        out_specs=pl.BlockSpec((blk, blk), lambda i, j: (j, i)),
        out_shape=jax.ShapeDtypeStruct((c, r), dtype),
        compiler_params=pltpu.CompilerParams(dimension_semantics=("parallel", "parallel")),
        name="transpose_cast",
    )(x)


def _peer_kernel(u_ref, vt_ref, ht_ref, g1_ref, g2_ref, thr_ref, xmid_ref, gfin_ref, y_ref, acc, act, gate):
    e = pl.program_id(1)
    eb, tb = act.shape
    n_i1 = eb // N_KEYS

    @pl.when(e == 0)
    def _():
        acc[...] = jnp.zeros_like(acc)

    act[...] = jnp.dot(u_ref[...], ht_ref[...], preferred_element_type=F32)

    def per_token_tile(ts, carry):
        tok = pl.ds(pl.multiple_of(ts * LANES, LANES), LANES)
        for il in range(n_i1):
            w = jnp.zeros((N_KEYS, LANES), F32)
            for h in range(PEER_HEADS):
                prod = g1_ref[h, il:il + 1, tok] * g2_ref[h, :, tok]
                w = w + jnp.where(prod >= thr_ref[h:h + 1, tok], prod, 0.0)
            rows = slice(il * N_KEYS, (il + 1) * N_KEYS)
            gate[rows, tok] = (w * jax.nn.gelu(act[rows, tok])).astype(BF16)
        return carry

    lax.fori_loop(0, tb // LANES, per_token_tile, 0)
    acc[...] += jnp.dot(vt_ref[...], gate[...], preferred_element_type=F32)

    @pl.when(e == pl.num_programs(1) - 1)
    def _():
        y_ref[...] = _rms(xmid_ref[...] + acc[...].T, gfin_ref[...])


def _peer(u_bf16, vt_bf16, ht, g1, g2, thr, xmid, gfin, tb, eb):
    n = xmid.shape[0]
    heads = g1.shape[0]
    return pl.pallas_call(
        _peer_kernel,
        grid=(n // tb, N_EXPERTS // eb),
        in_specs=[
            pl.BlockSpec((eb, D_MODEL), lambda t, e: (e, 0)),
            pl.BlockSpec((D_MODEL, eb), lambda t, e: (0, e)),
            pl.BlockSpec((D_MODEL, tb), lambda t, e: (0, t)),
            pl.BlockSpec((heads, eb // N_KEYS, tb), lambda t, e: (0, e, t)),
            pl.BlockSpec((heads, N_KEYS, tb), lambda t, e: (0, 0, t)),
            pl.BlockSpec((heads, tb), lambda t, e: (0, t)),
            pl.BlockSpec((tb, D_MODEL), lambda t, e: (t, 0)),
            pl.BlockSpec((1, D_MODEL), lambda t, e: (0, 0)),
        ],
        out_specs=pl.BlockSpec((tb, D_MODEL), lambda t, e: (t, 0)),
        out_shape=jax.ShapeDtypeStruct((n, D_MODEL), F32),
        scratch_shapes=[pltpu.VMEM((D_MODEL, tb), F32), pltpu.VMEM((eb, tb), F32), pltpu.VMEM((eb, tb), BF16)],
        compiler_params=pltpu.CompilerParams(
            dimension_semantics=("parallel", "arbitrary"), vmem_limit_bytes=VMEM_LIMIT),
        name="peer_dense",
    )(u_bf16, vt_bf16, ht, g1, g2, thr, xmid, gfin)


def _ffn_and_norm(oa, ob, x, params, tm, tb_topk, tb, eb):
    xmid, ht, st = _mix(oa, ob, x, params["ga"], params["gb"], params["wo"], params["gf"], params["wq"],
                        params["keys"], tm)
    g1, g2, thr = _peer_topk(st, tb_topk)
    return _peer(params["u"], params["vt"], ht, g1, g2, thr, xmid, params["gfin"], tb, eb)


def kernel(x_prompt, x_sample, cache_swa_kv, cache_dil_kv, attn_norm, w_in, sinks, out_norm_a, out_norm_b, w_out,
           ffn_norm, peer_w_q, peer_sub_keys, peer_u, peer_v, final_norm):
    depth = w_in.shape[0]
    assert depth == 1
    batch, seq, _ = x_prompt.shape
    n_s, s_len, _ = x_sample.shape
    assert s_len == 1 and seq % CHUNK == 0
    win_a = min(BLOCK, seq)
    win_b = min(CHUNK, seq)

    w_in_b = w_in[0].astype(BF16)
    g_attn = attn_norm[0][None]
    params = dict(
        ga=out_norm_a[0][None], gb=out_norm_b[0][None], wo=w_out[0].astype(BF16), gf=ffn_norm[0][None],
        wq=peer_w_q[0].astype(BF16),
        keys=peer_sub_keys[0].reshape(2 * PEER_HEADS, N_KEYS, PEER_QDIM // 2).astype(BF16),
        u=peer_u[0].astype(BF16), vt=_transpose_cast(peer_v[0], BF16, 1024), gfin=final_norm[None])
    sink_row = jnp.repeat(sinks[0].astype(F32), HEAD_DIM)[None]

    xp = x_prompt.reshape(batch * seq, D_MODEL)
    pos_p = jnp.tile(jnp.arange(seq), batch)
    qa, kva, qb, kvb = _project(xp, pos_p, g_attn, w_in_b, 512)
    oa, ob = _attn_prompt(qa, kva, qb, kvb, sink_row, batch, seq)
    y_prompt = _ffn_and_norm(oa, ob, xp, params, 256, 128, 512, 1024).reshape(batch, seq, D_MODEL)
    swa_p = kva.reshape(batch, seq, 2, KV_A, HEAD_DIM)[:, seq - win_a:][None]
    dil_p = kvb.reshape(batch, seq, 2, H_B, HEAD_DIM)[:, seq - win_b:][None]

    xs = x_sample.reshape(n_s, D_MODEL)
    pos_s = jnp.full((n_s,), PAST_LEN, jnp.int32)
    qa_s, kva_s, qb_s, kvb_s = _project(xs, pos_s, g_attn, w_in_b, n_s)
    l_a = cache_swa_kv.shape[2]
    l_b = cache_dil_kv.shape[2]
    cache_a = cache_swa_kv[0].reshape(n_s, l_a, 2 * K_A)
    cache_b = cache_dil_kv[0].reshape(n_s, l_b, 2 * Q_B)
    oa_s, ob_s, swa_new = _attn_sample(qa_s, kva_s, qb_s, kvb_s, cache_a, cache_b, sinks[0], 8)
    y_sample = _ffn_and_norm(oa_s, ob_s, xs, params, n_s, 128, 128, 1024).reshape(n_s, 1, D_MODEL)
    swa_s = swa_new.reshape(1, n_s, l_a, 2, KV_A, HEAD_DIM)
    dil_s = _shift_cache(cache_b, kvb_s.reshape(n_s, 1, 2 * Q_B), 1024).reshape(1, n_s, l_b, 2, H_B, HEAD_DIM)
    return y_prompt, y_sample, swa_p, dil_p, swa_s, dil_s
```

```python
import functools

import jax
import jax.numpy as jnp
from jax import lax
from jax.experimental import pallas as pl
from jax.experimental.pallas import tpu as pltpu

F32 = jnp.float32
BF16 = jnp.bfloat16

D_MODEL = 1024
HEAD_DIM = 64
PAST_LEN = 8192
H_A = 8
KV_A = 2
H_B = 8
Q_A = H_A * HEAD_DIM
K_A = KV_A * HEAD_DIM
Q_B = H_B * HEAD_DIM
D_IN = Q_A + 2 * K_A + 3 * Q_B
ROT_DIM = HEAD_DIM // 4
ROPE_THETA = 500000.0
BLOCK = 128
DILATIONS = (1, 4, 16)
CHUNK = BLOCK * DILATIONS[-1]
EPS = 1e-6
LANES = 128
SUBLANES = 8
NEG = -1e30

N_KEYS = 128
PEER_HEADS = 8
PEER_TOPK = 16
PEER_QDIM = 256
N_EXPERTS = N_KEYS * N_KEYS

VMEM_LIMIT = 56 * 1024 * 1024


def _rms(x, g):
    return (x * lax.rsqrt(jnp.mean(x * x, axis=-1, keepdims=True) + EPS)) * g


def _rope_tables(pos):
    inv = ROPE_THETA ** (-jnp.arange(0, ROT_DIM, 2, dtype=F32) / ROT_DIM)
    ang = pos.astype(F32)[:, None] * inv[None, :]
    cos, sin = jnp.cos(ang), jnp.sin(ang)
    t = pos.shape[0]
    half = ROT_DIM // 2
    rest = HEAD_DIM - ROT_DIM
    c = jnp.concatenate([cos, cos, jnp.ones((t, rest), F32)], axis=-1)
    s1 = jnp.concatenate([-sin, jnp.zeros((t, half + rest), F32)], axis=-1)
    s2 = jnp.concatenate([jnp.zeros((t, half), F32), sin, jnp.zeros((t, rest), F32)], axis=-1)
    rep = LANES // HEAD_DIM
    return jnp.tile(c, (1, rep)), jnp.tile(s1, (1, rep)), jnp.tile(s2, (1, rep))


def _proj_kernel(x_ref, g_ref, w_ref, c_ref, s1_ref, s2_ref, qa_ref, kva_ref, qb_ref, kvb_ref):
    h = _rms(x_ref[...], g_ref[...])
    z = jnp.dot(h.astype(BF16), w_ref[...], preferred_element_type=F32)
    c, s1, s2 = c_ref[...], s1_ref[...], s2_ref[...]
    half = ROT_DIM // 2

    def chunk(i, rot):
        zc = z[:, i * LANES:(i + 1) * LANES]
        if not rot:
            return zc
        return zc * c + pltpu.roll(zc, LANES - half, 1) * s1 + pltpu.roll(zc, half, 1) * s2

    nq = Q_A // LANES
    for i in range(nq):
        qa_ref[:, i * LANES:(i + 1) * LANES] = chunk(i, True)
    kva_ref[:, 0:LANES] = chunk(nq, True)
    kva_ref[:, LANES:2 * LANES] = chunk(nq + 1, False)
    base = nq + 2
    for i in range(nq):
        qb_ref[:, i * LANES:(i + 1) * LANES] = chunk(base + i, True)
    for i in range(nq):
        kvb_ref[:, i * LANES:(i + 1) * LANES] = chunk(base + nq + i, True)
    for i in range(nq):
        kvb_ref[:, Q_B + i * LANES:Q_B + (i + 1) * LANES] = chunk(base + 2 * nq + i, False)


def _project(x, pos, g, w_bf16, tm):
    n = x.shape[0]
    c, s1, s2 = _rope_tables(pos)
    row = lambda width: pl.BlockSpec((tm, width), lambda i: (i, 0))
    full = lambda a: pl.BlockSpec(a.shape, lambda i: (0, 0))
    return pl.pallas_call(
        _proj_kernel,
        grid=(n // tm,),
        in_specs=[row(D_MODEL), full(g), full(w_bf16), row(LANES), row(LANES), row(LANES)],
        out_specs=[row(Q_A), row(2 * K_A), row(Q_B), row(2 * Q_B)],
        out_shape=[jax.ShapeDtypeStruct((n, Q_A), F32), jax.ShapeDtypeStruct((n, 2 * K_A), F32),
                   jax.ShapeDtypeStruct((n, Q_B), F32), jax.ShapeDtypeStruct((n, 2 * Q_B), F32)],
        compiler_params=pltpu.CompilerParams(dimension_semantics=("parallel",), vmem_limit_bytes=VMEM_LIMIT),
        name="in_proj",
    )(x, g, w_bf16, c, s1, s2)


def _band_mask(has_prev):
    r = lax.broadcasted_iota(jnp.int32, (BLOCK, 2 * BLOCK), 0)
    k = lax.broadcasted_iota(jnp.int32, (BLOCK, 2 * BLOCK), 1)
    dist = r + BLOCK - k
    band = (dist >= 0) & (dist <= BLOCK)
    return band & ((k >= BLOCK) | has_prev)


def _half_softmax_pv(qm, k, vm, mask):
    s = lax.dot_general(qm.astype(BF16), k.astype(BF16), (((1,), (1,)), ((), ())),
                        preferred_element_type=F32) * (HEAD_DIM ** -0.5)
    s = jnp.where(mask, s, NEG)
    m = jnp.max(s, axis=-1, keepdims=True)
    p = jnp.exp(s - m)
    l = jnp.sum(p, axis=-1, keepdims=True)
    o = jnp.dot(p.astype(BF16), vm.astype(BF16), preferred_element_type=F32)
    return o, m, l


def _attn_prompt_kernel(qa_ref, kva_c_ref, kva_p_ref, qb_ref, kb_c_ref, kb_p_ref, vb_c_ref, vb_p_ref, sink_ref,
                        oa_ref, ob_ref, kva_buf, kb_buf, vb_buf, o_buf, lse_buf):
    hp = pl.program_id(1)
    ck = pl.program_id(2)
    kva_buf[0:CHUNK] = kva_p_ref[0]
    kva_buf[CHUNK:2 * CHUNK] = kva_c_ref[0]
    kb_buf[0:CHUNK] = kb_p_ref[0]
    kb_buf[CHUNK:2 * CHUNK] = kb_c_ref[0]
    vb_buf[0:CHUNK] = vb_p_ref[0]
    vb_buf[CHUNK:2 * CHUNK] = vb_c_ref[0]

    lane = lax.broadcasted_iota(jnp.int32, (1, LANES), 1)
    in_half = [lane < HEAD_DIM, lane >= HEAD_DIM]
    kvh = hp // (H_A // KV_A // 2)
    kv_half = (lane >= kvh * HEAD_DIM) & (lane < (kvh + 1) * HEAD_DIM)
    sink = sink_ref[...]
    nblk = CHUNK // BLOCK

    def body(i, carry):
        start = pl.multiple_of(i * BLOCK, BLOCK)
        has_prev = (ck > 0) | (i > 0)
        mask = _band_mask(has_prev)
        q = qa_ref[0, pl.ds(start, BLOCK), :]
        kv = kva_buf[pl.ds(CHUNK - BLOCK + start, 2 * BLOCK), :]
        k = kv[:, 0:LANES]
        v = jnp.where(kv_half, kv[:, LANES:2 * LANES], 0.0)
        q_sw = pltpu.roll(q, HEAD_DIM, 1)
        acc = jnp.zeros((BLOCK, LANES), F32)
        for a in range(2):
            q_al = jnp.where(kvh == a, q, q_sw)
            qm = jnp.where(kv_half, q_al, 0.0)
            o, m, l = _half_softmax_pv(qm, k, v, mask)
            o = jnp.where(kvh == a, o, pltpu.roll(o, HEAD_DIM, 1))
            acc = acc + jnp.where(in_half[a], o / (l + jnp.exp(sink - m)), 0.0)
        oa_ref[0, pl.ds(start, BLOCK), :] = acc

        for bi, d in enumerate(DILATIONS):
            j, r = i // d, i % d
            qstart = j * (BLOCK * d) + r
            has_prev_b = (ck > 0) | (j > 0)
            mask_b = _band_mask(has_prev_b)
            if d == 1:
                qd = qb_ref[0, pl.ds(pl.multiple_of(qstart, BLOCK), BLOCK), :]
                kd = kb_buf[pl.ds(pl.multiple_of(CHUNK - BLOCK + qstart, BLOCK), 2 * BLOCK), :]
                vd = vb_buf[pl.ds(pl.multiple_of(CHUNK - BLOCK + qstart, BLOCK), 2 * BLOCK), :]
            else:
                qd = qb_ref[0, pl.ds(qstart, BLOCK, stride=d), :]
                kd = kb_buf[pl.ds(CHUNK - BLOCK * d + qstart, 2 * BLOCK, stride=d), :]
                vd = vb_buf[pl.ds(CHUNK - BLOCK * d + qstart, 2 * BLOCK, stride=d), :]
            o_acc = jnp.zeros((BLOCK, LANES), F32)
            lse_acc = jnp.zeros((BLOCK, LANES), F32)
            for a in range(2):
                qm = jnp.where(in_half[a], qd, 0.0)
                vm = jnp.where(in_half[a], vd, 0.0)
                o, m, l = _half_softmax_pv(qm, kd, vm, mask_b)
                o_acc = o_acc + o / l
                lse_acc = lse_acc + jnp.where(in_half[a], m + jnp.log(l), 0.0)
            if d == 1:
                o_buf[bi, pl.ds(pl.multiple_of(qstart, BLOCK), BLOCK), :] = o_acc
                lse_buf[bi, pl.ds(pl.multiple_of(qstart, BLOCK), BLOCK), :] = lse_acc
            else:
                o_buf[bi, pl.ds(qstart, BLOCK, stride=d), :] = o_acc
                lse_buf[bi, pl.ds(qstart, BLOCK, stride=d), :] = lse_acc
        return carry

    lax.fori_loop(0, nblk, body, 0)

    l0, l1, l2 = lse_buf[0], lse_buf[1], lse_buf[2]
    mx = jnp.maximum(jnp.maximum(l0, l1), l2)
    w0, w1, w2 = jnp.exp(l0 - mx), jnp.exp(l1 - mx), jnp.exp(l2 - mx)
    tot = w0 + w1 + w2
    ob_ref[0] = (w0 / tot) * o_buf[0] + (w1 / tot) * o_buf[1] + (w2 / tot) * o_buf[2]


def _attn_prompt(qa, kva, qb, kvb, sink_row, batch, seq):
    qa = qa.reshape(batch, seq, Q_A)
    kva = kva.reshape(batch, seq, 2 * K_A)
    qb = qb.reshape(batch, seq, Q_B)
    kvb = kvb.reshape(batch, seq, 2 * Q_B)
    npair = Q_A // LANES
    cur = lambda b, h, c: (b, c, h)
    prev = lambda b, h, c: (b, jnp.maximum(c - 1, 0), h)
    blk = lambda width, fn: pl.BlockSpec((1, CHUNK, width), fn)
    oa, ob = pl.pallas_call(
        _attn_prompt_kernel,
        grid=(batch, npair, seq // CHUNK),
        in_specs=[
            blk(LANES, cur),
            blk(2 * K_A, lambda b, h, c: (b, c, 0)),
            blk(2 * K_A, lambda b, h, c: (b, jnp.maximum(c - 1, 0), 0)),
            blk(LANES, cur),
            blk(LANES, cur),
            blk(LANES, prev),
            blk(LANES, lambda b, h, c: (b, c, npair + h)),
            blk(LANES, lambda b, h, c: (b, jnp.maximum(c - 1, 0), npair + h)),
            pl.BlockSpec((1, LANES), lambda b, h, c: (0, h)),
        ],
        out_specs=[blk(LANES, cur), blk(LANES, cur)],
        out_shape=[jax.ShapeDtypeStruct((batch, seq, Q_A), F32), jax.ShapeDtypeStruct((batch, seq, Q_B), F32)],
        scratch_shapes=[
            pltpu.VMEM((2 * CHUNK, 2 * K_A), F32),
            pltpu.VMEM((2 * CHUNK, LANES), F32),
            pltpu.VMEM((2 * CHUNK, LANES), F32),
            pltpu.VMEM((len(DILATIONS), CHUNK, LANES), F32),
            pltpu.VMEM((len(DILATIONS), CHUNK, LANES), F32),
        ],
        compiler_params=pltpu.CompilerParams(
            dimension_semantics=("parallel", "parallel", "arbitrary"), vmem_limit_bytes=VMEM_LIMIT),
        name="attn_prompt",
    )(qa, kva, kva, qb, kvb, kvb, kvb, kvb, sink_row)
    return oa.reshape(batch * seq, Q_A), ob.reshape(batch * seq, Q_B)


def _transpose_tail_kernel(x_ref, o_ref):
    o_ref[0] = x_ref[0].T


def _transpose_tail(x, rows):
    b, t, w = x.shape
    blk = min(rows, 512)
    first = (t - rows) // blk
    return pl.pallas_call(
        _transpose_tail_kernel,
        grid=(b, rows // blk),
        in_specs=[pl.BlockSpec((1, blk, w), lambda i, j: (i, first + j, 0))],
        out_specs=pl.BlockSpec((1, w, blk), lambda i, j: (i, 0, j)),
        out_shape=jax.ShapeDtypeStruct((b, w, rows), x.dtype),
        compiler_params=pltpu.CompilerParams(dimension_semantics=("parallel", "parallel")),
        name="transpose_tail",
    )(x)


def _sample_attn_kernel(sink_ref, swa_ref, dil_ref, qa_ref, kva_ref, qb_ref, kvb_ref,
                        swa_out_ref, dil_out_ref, oa_ref, ob_ref):
    n = pl.program_id(0)
    win = swa_ref.shape[-1]
    span = dil_ref.shape[-1]
    scale = HEAD_DIM ** -0.5

    @pl.when(n == 0)
    def _():
        oa_ref[...] = jnp.zeros_like(oa_ref)
        ob_ref[...] = jnp.zeros_like(ob_ref)

    mine = lax.broadcasted_iota(jnp.int32, (1, qa_ref.shape[1]), 1) == n

    def pick(ref, start):
        return jnp.sum(jnp.where(mine, ref[start:start + HEAD_DIM, :], 0.0), axis=1, keepdims=True)

    def put(ref, start, col):
        ref[start:start + HEAD_DIM, :] = jnp.where(mine, col, ref[start:start + HEAD_DIM, :])

    def attend(kt, vt, q, knew, vnew, weight, self_weight, extra):
        s = jnp.sum(kt * q, axis=0, keepdims=True) * scale
        s_self = jnp.sum(knew * q, axis=0, keepdims=True) * scale
        if weight is not None:
            s = jnp.where(weight > 0.0, s, NEG)
        m = jnp.maximum(jnp.max(s, axis=1, keepdims=True), s_self)
        p = jnp.exp(s - m)
        if weight is not None:
            p = p * weight
        p_self = self_weight * jnp.exp(s_self - m)
        l = jnp.sum(p, axis=1, keepdims=True) + p_self
        if extra is not None:
            l = l + jnp.exp(extra - m)
        return (jnp.sum(vt * p, axis=1, keepdims=True) + vnew * p_self) / l

    def shifted(x, new):
        width = x.shape[-1]
        rolled = pltpu.roll(x, width - 1, 1)
        last = lax.broadcasted_iota(jnp.int32, (1, width), 1) == width - 1
        return jnp.where(last, new, rolled)

    group = H_A // KV_A
    for hq in range(H_A):
        kvh = hq // group
        o = attend(swa_ref[0, 0, kvh], swa_ref[0, 1, kvh], pick(qa_ref, hq * HEAD_DIM),
                   pick(kva_ref, kvh * HEAD_DIM), pick(kva_ref, K_A + kvh * HEAD_DIM), None, 1.0, sink_ref[hq])
        put(oa_ref, hq * HEAD_DIM, o)
    for kv in range(2):
        for h in range(KV_A):
            swa_out_ref[0, kv, h] = shifted(swa_ref[0, kv, h], pick(kva_ref, kv * K_A + h * HEAD_DIM))

    pos = lax.broadcasted_iota(jnp.int32, (1, span), 1)
    weight = jnp.zeros((1, span), F32)
    for d in DILATIONS:
        weight = weight + jnp.where((pos % d == 0) & (pos >= span - BLOCK * d), 1.0, 0.0)
    for h in range(H_B):
        knew = pick(kvb_ref, h * HEAD_DIM)
        vnew = pick(kvb_ref, Q_B + h * HEAD_DIM)
        o = attend(dil_ref[0, 0, h], dil_ref[0, 1, h], pick(qb_ref, h * HEAD_DIM), knew, vnew, weight,
                   float(len(DILATIONS)), None)
        put(ob_ref, h * HEAD_DIM, o)
        dil_out_ref[0, 0, h] = shifted(dil_ref[0, 0, h], knew)
        dil_out_ref[0, 1, h] = shifted(dil_ref[0, 1, h], vnew)


def _sample_attn(sinks, swa, dil, qa_t, kva_t, qb_t, kvb_t):
    n = swa.shape[0]
    assert swa.shape[-1] == BLOCK and dil.shape[-1] == CHUNK
    per_sample = lambda a: pl.BlockSpec((1,) + a.shape[1:], lambda i: (i, 0, 0, 0, 0))
    full = lambda a: pl.BlockSpec(a.shape, lambda i: (0, 0))
    o_shape = jax.ShapeDtypeStruct((Q_A, n), F32)
    return pl.pallas_call(
        _sample_attn_kernel,
        grid=(n,),
        in_specs=[pl.BlockSpec(memory_space=pltpu.SMEM), per_sample(swa), per_sample(dil),
                  full(qa_t), full(kva_t), full(qb_t), full(kvb_t)],
        out_specs=[per_sample(swa), per_sample(dil), full(qa_t), full(qb_t)],
        out_shape=[jax.ShapeDtypeStruct(swa.shape, F32), jax.ShapeDtypeStruct(dil.shape, F32), o_shape, o_shape],
        compiler_params=pltpu.CompilerParams(dimension_semantics=("arbitrary",), vmem_limit_bytes=VMEM_LIMIT),
        name="sample_attn",
    )(sinks, swa, dil, qa_t, kva_t, qb_t, kvb_t)


def _mix_kernel(oa_ref, ob_ref, x_ref, ga_ref, gb_ref, wo_ref, gf_ref, wq_ref, keys_ref, xmid_ref, ht_ref, st_ref):
    o = jnp.concatenate([_rms(oa_ref[...], ga_ref[...]), _rms(ob_ref[...], gb_ref[...])], axis=-1)
    xm = x_ref[...] + jnp.dot(o.astype(BF16), wo_ref[...], preferred_element_type=F32)
    xmid_ref[...] = xm
    h = _rms(xm, gf_ref[...])
    hb = h.astype(BF16)
    ht_ref[...] = h.T.astype(BF16)
    q = jnp.dot(hb, wq_ref[...], preferred_element_type=F32).astype(BF16)
    half = PEER_QDIM // 2
    for i in range(2 * PEER_HEADS):
        st_ref[i] = lax.dot_general(keys_ref[i], q[:, i * half:(i + 1) * half], (((1,), (1,)), ((), ())),
                                    preferred_element_type=F32)


def _mix(oa, ob, x, ga, gb, wo, gf, wq, keys, tm):
    n = x.shape[0]
    nk = keys.shape[0]
    row = lambda width: pl.BlockSpec((tm, width), lambda i: (i, 0))
    full = lambda a: pl.BlockSpec(a.shape, lambda i: (0,) * a.ndim)
    return pl.pallas_call(
        _mix_kernel,
        grid=(n // tm,),
        in_specs=[row(Q_A), row(Q_B), row(D_MODEL), full(ga), full(gb), full(wo), full(gf), full(wq), full(keys)],
        out_specs=[row(D_MODEL), pl.BlockSpec((D_MODEL, tm), lambda i: (0, i)),
                   pl.BlockSpec((nk, N_KEYS, tm), lambda i: (0, 0, i))],
        out_shape=[jax.ShapeDtypeStruct((n, D_MODEL), F32), jax.ShapeDtypeStruct((D_MODEL, n), BF16),
                   jax.ShapeDtypeStruct((nk, N_KEYS, n), F32)],
        compiler_params=pltpu.CompilerParams(dimension_semantics=("parallel",), vmem_limit_bytes=VMEM_LIMIT),
        name="mix_out",
    )(oa, ob, x, ga, gb, wo, gf, wq, keys)


_CAND = [(i, j) for i in range(PEER_TOPK) for j in range(PEER_TOPK) if (i + 1) * (j + 1) <= PEER_TOPK]


def _topk_kernel(s_ref, g1_ref, g2_ref, thr_ref, a_scr, b_scr):
    def per_head(h, carry):
        for p, (g_ref, scr) in enumerate(((g1_ref, a_scr), (g2_ref, b_scr))):
            s = s_ref[2 * h + p]
            e = jnp.exp(s - jnp.max(s, axis=0, keepdims=True))
            g_ref[h] = e
            cur = e
            for k in range(PEER_TOPK):
                mk = jnp.max(cur, axis=0, keepdims=True)
                scr[k, pl.ds(h, 1), :] = jnp.maximum(mk, 0.0)
                cur = jnp.where(cur == mk, -1.0, cur)
        return carry

    lax.fori_loop(0, PEER_HEADS, per_head, 0)

    a = [a_scr[k] for k in range(PEER_TOPK)]
    b = [b_scr[k] for k in range(PEER_TOPK)]
    cand = [a[i] * b[j] for i, j in _CAND]
    cur = list(cand)
    z = jnp.zeros_like(a[0])
    theta = z
    for k in range(PEER_TOPK):
        mk = functools.reduce(jnp.maximum, cur)
        z = z + jnp.maximum(mk, 0.0)
        theta = jnp.where(mk > 0.0, mk, theta)
        cur = [jnp.where(c == mk, -1.0, c) for c in cur]
    rz = 1.0 / z
    thr = jnp.full_like(z, 2.0)
    for (i, j), c in zip(_CAND, cand):
        thr = jnp.minimum(thr, jnp.where(c >= theta, (a[i] * rz) * b[j], 2.0))
    thr_ref[...] = thr
    for h in range(PEER_HEADS):
        g1_ref[h] = g1_ref[h] * rz[h:h + 1, :]


def _peer_topk(st, tb):
    nk, _, n = st.shape
    heads = nk // 2
    g_spec = pl.BlockSpec((heads, N_KEYS, tb), lambda i: (0, 0, i))
    g_shape = jax.ShapeDtypeStruct((heads, N_KEYS, n), F32)
    return pl.pallas_call(
        _topk_kernel,
        grid=(n // tb,),
        in_specs=[pl.BlockSpec((nk, N_KEYS, tb), lambda i: (0, 0, i))],
        out_specs=[g_spec, g_spec, pl.BlockSpec((heads, tb), lambda i: (0, i))],
        out_shape=[g_shape, g_shape, jax.ShapeDtypeStruct((heads, n), F32)],
        scratch_shapes=[pltpu.VMEM((PEER_TOPK, heads, tb), F32), pltpu.VMEM((PEER_TOPK, heads, tb), F32)],
        compiler_params=pltpu.CompilerParams(dimension_semantics=("parallel",), vmem_limit_bytes=VMEM_LIMIT),
        name="peer_topk",
    )(st)


def _transpose_cast_kernel(x_ref, o_ref):
    o_ref[...] = x_ref[...].T.astype(o_ref.dtype)


def _transpose_cast(x, dtype, blk):
    r, c = x.shape
    return pl.pallas_call(
        _transpose_cast_kernel,
        grid=(r // blk, c // blk),
        in_specs=[pl.BlockSpec((blk, blk), lambda i, j: (i, j))],
        out_specs=pl.BlockSpec((blk, blk), lambda i, j: (j, i)),
        out_shape=jax.ShapeDtypeStruct((c, r), dtype),
        compiler_params=pltpu.CompilerParams(dimension_semantics=("parallel", "parallel")),
        name="transpose_cast",
    )(x)


def _peer_kernel(u_ref, vt_ref, ht_ref, g1_ref, g2_ref, thr_ref, xmid_ref, gfin_ref, y_ref, acc, act, gate):
    e = pl.program_id(1)
    n_i1 = u_ref.shape[0] // N_KEYS
    tt = act.shape[2]
    nt = ht_ref.shape[1] // tt

    @pl.when(e == 0)
    def _():
        acc[...] = jnp.zeros_like(acc)

    def act_matmul(t):
        act[t % 2] = jnp.dot(u_ref[...], ht_ref[:, t * tt:(t + 1) * tt], preferred_element_type=F32)

    def out_matmul(t):
        acc[:, t * tt:(t + 1) * tt] += jnp.dot(vt_ref[...], gate[t % 2], preferred_element_type=F32)

    def gating(t):
        for sub in range(tt // LANES):
            tok = slice(t * tt + sub * LANES, t * tt + (sub + 1) * LANES)
            cols = slice(sub * LANES, (sub + 1) * LANES)
            for il in range(n_i1):
                w = jnp.zeros((N_KEYS, LANES), F32)
                for h in range(PEER_HEADS):
                    prod = g1_ref[h, il:il + 1, tok] * g2_ref[h, :, tok]
                    w = w + jnp.where(prod >= thr_ref[h:h + 1, tok], prod, 0.0)
                rows = slice(il * N_KEYS, (il + 1) * N_KEYS)
                gate[t % 2, rows, cols] = (w * jax.nn.gelu(act[t % 2, rows, cols])).astype(BF16)

    act_matmul(0)
    for t in range(nt):
        if t + 1 < nt:
            act_matmul(t + 1)
        gating(t)
        if t >= 1:
            out_matmul(t - 1)
    out_matmul(nt - 1)

    @pl.when(e == pl.num_programs(1) - 1)
    def _():
        y_ref[...] = _rms(xmid_ref[...] + acc[...].T, gfin_ref[...])


def _peer(u_bf16, vt_bf16, ht, g1, g2, thr, xmid, gfin, tb, eb):
    n = xmid.shape[0]
    heads = g1.shape[0]
    tt = min(tb, 2 * LANES)
    return pl.pallas_call(
        _peer_kernel,
        grid=(n // tb, N_EXPERTS // eb),
        in_specs=[
            pl.BlockSpec((eb, D_MODEL), lambda t, e: (e, 0)),
            pl.BlockSpec((D_MODEL, eb), lambda t, e: (0, e)),
            pl.BlockSpec((D_MODEL, tb), lambda t, e: (0, t)),
            pl.BlockSpec((heads, eb // N_KEYS, tb), lambda t, e: (0, e, t)),
            pl.BlockSpec((heads, N_KEYS, tb), lambda t, e: (0, 0, t)),
            pl.BlockSpec((heads, tb), lambda t, e: (0, t)),
            pl.BlockSpec((tb, D_MODEL), lambda t, e: (t, 0)),
            pl.BlockSpec((1, D_MODEL), lambda t, e: (0, 0)),
        ],
        out_specs=pl.BlockSpec((tb, D_MODEL), lambda t, e: (t, 0)),
        out_shape=jax.ShapeDtypeStruct((n, D_MODEL), F32),
        scratch_shapes=[pltpu.VMEM((D_MODEL, tb), F32), pltpu.VMEM((2, eb, tt), F32), pltpu.VMEM((2, eb, tt), BF16)],
        compiler_params=pltpu.CompilerParams(
            dimension_semantics=("parallel", "arbitrary"), vmem_limit_bytes=VMEM_LIMIT),
        name="peer_dense",
    )(u_bf16, vt_bf16, ht, g1, g2, thr, xmid, gfin)


def _ffn_and_norm(oa, ob, x, params, tm, tb_topk, tb, eb):
    xmid, ht, st = _mix(oa, ob, x, params["ga"], params["gb"], params["wo"], params["gf"], params["wq"],
                        params["keys"], tm)
    g1, g2, thr = _peer_topk(st, tb_topk)
    return _peer(params["u"], params["vt"], ht, g1, g2, thr, xmid, params["gfin"], tb, eb)


def kernel(x_prompt, x_sample, cache_swa_kv, cache_dil_kv, attn_norm, w_in, sinks, out_norm_a, out_norm_b, w_out,
           ffn_norm, peer_w_q, peer_sub_keys, peer_u, peer_v, final_norm):
    depth = w_in.shape[0]
    assert depth == 1
    batch, seq, _ = x_prompt.shape
    n_s, s_len, _ = x_sample.shape
    assert s_len == 1 and seq % CHUNK == 0
    win_a = min(BLOCK, seq)
    win_b = min(CHUNK, seq)

    w_in_b = w_in[0].astype(BF16)
    g_attn = attn_norm[0][None]
    params = dict(
        ga=out_norm_a[0][None], gb=out_norm_b[0][None], wo=w_out[0].astype(BF16), gf=ffn_norm[0][None],
        wq=peer_w_q[0].astype(BF16),
        keys=peer_sub_keys[0].reshape(2 * PEER_HEADS, N_KEYS, PEER_QDIM // 2).astype(BF16),
        u=peer_u[0].astype(BF16), vt=_transpose_cast(peer_v[0], BF16, 1024), gfin=final_norm[None])
    sink_row = jnp.repeat(sinks[0].astype(F32), HEAD_DIM)[None]
    to_rows = lambda a: jnp.transpose(a, (0, 4, 1, 2, 3))[None]
    to_lanes = lambda a: jnp.transpose(a[0], (0, 2, 3, 4, 1))

    xp = x_prompt.reshape(batch * seq, D_MODEL)
    pos_p = jnp.tile(jnp.arange(seq), batch)
    qa, kva, qb, kvb = _project(xp, pos_p, g_attn, w_in_b, 512)
    oa, ob = _attn_prompt(qa, kva, qb, kvb, sink_row, batch, seq)
    y_prompt = _ffn_and_norm(oa, ob, xp, params, 256, 128, 1024, 1024).reshape(batch, seq, D_MODEL)
    swa_p = to_rows(_transpose_tail(kva.reshape(batch, seq, 2 * K_A), win_a).reshape(batch, 2, KV_A, HEAD_DIM, win_a))
    dil_p = to_rows(_transpose_tail(kvb.reshape(batch, seq, 2 * Q_B), win_b).reshape(batch, 2, H_B, HEAD_DIM, win_b))

    xs = x_sample.reshape(n_s, D_MODEL)
    pos_s = jnp.full((n_s,), PAST_LEN, jnp.int32)
    qa_s, kva_s, qb_s, kvb_s = _project(xs, pos_s, g_attn, w_in_b, n_s)
    swa_new, dil_new, oa_t, ob_t = _sample_attn(sinks[0].astype(F32), to_lanes(cache_swa_kv), to_lanes(cache_dil_kv),
                                                qa_s.T, kva_s.T, qb_s.T, kvb_s.T)
    y_sample = _ffn_and_norm(oa_t.T, ob_t.T, xs, params, n_s, 128, 128, 1024).reshape(n_s, 1, D_MODEL)
    return y_prompt, y_sample, swa_p, dil_p, to_rows(swa_new), to_rows(dil_new)
```

```python
import functools

import jax
import jax.numpy as jnp
from jax import lax
from jax.experimental import pallas as pl
from jax.experimental.pallas import tpu as pltpu

F32 = jnp.float32
BF16 = jnp.bfloat16

D_MODEL = 1024
HEAD_DIM = 64
PAST_LEN = 8192
H_A = 8
KV_A = 2
H_B = 8
Q_A = H_A * HEAD_DIM
K_A = KV_A * HEAD_DIM
Q_B = H_B * HEAD_DIM
D_IN = Q_A + 2 * K_A + 3 * Q_B
ROT_DIM = HEAD_DIM // 4
ROPE_THETA = 500000.0
BLOCK = 128
DILATIONS = (1, 4, 16)
CHUNK = BLOCK * DILATIONS[-1]
EPS = 1e-6
LANES = 128
SUBLANES = 8
NEG = -1e30

N_KEYS = 128
PEER_HEADS = 8
PEER_TOPK = 16
PEER_QDIM = 256
N_EXPERTS = N_KEYS * N_KEYS

VMEM_LIMIT = 56 * 1024 * 1024


def _rms(x, g):
    return (x * lax.rsqrt(jnp.mean(x * x, axis=-1, keepdims=True) + EPS)) * g


def _rope_tables(pos):
    inv = ROPE_THETA ** (-jnp.arange(0, ROT_DIM, 2, dtype=F32) / ROT_DIM)
    ang = pos.astype(F32)[:, None] * inv[None, :]
    cos, sin = jnp.cos(ang), jnp.sin(ang)
    t = pos.shape[0]
    half = ROT_DIM // 2
    rest = HEAD_DIM - ROT_DIM
    c = jnp.concatenate([cos, cos, jnp.ones((t, rest), F32)], axis=-1)
    s1 = jnp.concatenate([-sin, jnp.zeros((t, half + rest), F32)], axis=-1)
    s2 = jnp.concatenate([jnp.zeros((t, half), F32), sin, jnp.zeros((t, rest), F32)], axis=-1)
    rep = LANES // HEAD_DIM
    return jnp.tile(c, (1, rep)), jnp.tile(s1, (1, rep)), jnp.tile(s2, (1, rep))


def _proj_kernel(x_ref, g_ref, w_ref, c_ref, s1_ref, s2_ref, qa_ref, kva_ref, qb_ref, kvb_ref):
    h = _rms(x_ref[...], g_ref[...])
    z = jnp.dot(h.astype(BF16), w_ref[...], preferred_element_type=F32)
    c, s1, s2 = c_ref[...], s1_ref[...], s2_ref[...]
    half = ROT_DIM // 2

    def chunk(i, rot):
        zc = z[:, i * LANES:(i + 1) * LANES]
        if not rot:
            return zc
        return zc * c + pltpu.roll(zc, LANES - half, 1) * s1 + pltpu.roll(zc, half, 1) * s2

    nq = Q_A // LANES
    for i in range(nq):
        qa_ref[:, i * LANES:(i + 1) * LANES] = chunk(i, True)
    kva_ref[:, 0:LANES] = chunk(nq, True)
    kva_ref[:, LANES:2 * LANES] = chunk(nq + 1, False)
    base = nq + 2
    for i in range(nq):
        qb_ref[:, i * LANES:(i + 1) * LANES] = chunk(base + i, True)
    for i in range(nq):
        kvb_ref[:, i * LANES:(i + 1) * LANES] = chunk(base + nq + i, True)
    for i in range(nq):
        kvb_ref[:, Q_B + i * LANES:Q_B + (i + 1) * LANES] = chunk(base + 2 * nq + i, False)


def _project(x, pos, g, w_bf16, tm):
    n = x.shape[0]
    c, s1, s2 = _rope_tables(pos)
    row = lambda width: pl.BlockSpec((tm, width), lambda i: (i, 0))
    full = lambda a: pl.BlockSpec(a.shape, lambda i: (0, 0))
    return pl.pallas_call(
        _proj_kernel,
        grid=(n // tm,),
        in_specs=[row(D_MODEL), full(g), full(w_bf16), row(LANES), row(LANES), row(LANES)],
        out_specs=[row(Q_A), row(2 * K_A), row(Q_B), row(2 * Q_B)],
        out_shape=[jax.ShapeDtypeStruct((n, Q_A), F32), jax.ShapeDtypeStruct((n, 2 * K_A), F32),
                   jax.ShapeDtypeStruct((n, Q_B), F32), jax.ShapeDtypeStruct((n, 2 * Q_B), F32)],
        compiler_params=pltpu.CompilerParams(dimension_semantics=("parallel",), vmem_limit_bytes=VMEM_LIMIT),
        name="in_proj",
    )(x, g, w_bf16, c, s1, s2)


def _band_bias(with_prev):
    r = lax.broadcasted_iota(jnp.int32, (BLOCK, 2 * BLOCK), 0)
    k = lax.broadcasted_iota(jnp.int32, (BLOCK, 2 * BLOCK), 1)
    dist = r + BLOCK - k
    ok = (dist >= 0) & (dist <= BLOCK)
    if not with_prev:
        ok = ok & (k >= BLOCK)
    return jnp.where(ok, 0.0, NEG)


def _half_softmax_pv(qm, k, vm, bias):
    s = lax.dot_general(qm.astype(BF16), k.astype(BF16), (((1,), (1,)), ((), ())), preferred_element_type=F32) + bias
    m = jnp.max(s, axis=-1, keepdims=True)
    p = jnp.exp(s - m)
    l = jnp.sum(p, axis=-1, keepdims=True)
    o = jnp.dot(p.astype(BF16), vm.astype(BF16), preferred_element_type=F32)
    return o, m, l


def _attn_prompt_kernel(qa_ref, kva_c_ref, kva_p_ref, qb_ref, kb_c_ref, kb_p_ref, vb_c_ref, vb_p_ref, sink_ref,
                        oa_ref, ob_ref, kva_buf, kb_buf, vb_buf, o_buf, lse_buf, bias_buf):
    hp = pl.program_id(1)
    ck = pl.program_id(2)
    bias_buf[0] = _band_bias(False)
    bias_buf[1] = _band_bias(True)
    kva_buf[0:CHUNK] = kva_p_ref[0]
    kva_buf[CHUNK:2 * CHUNK] = kva_c_ref[0]
    kb_buf[0:CHUNK] = kb_p_ref[0]
    kb_buf[CHUNK:2 * CHUNK] = kb_c_ref[0]
    vb_buf[0:CHUNK] = vb_p_ref[0]
    vb_buf[CHUNK:2 * CHUNK] = vb_c_ref[0]

    lane = lax.broadcasted_iota(jnp.int32, (1, LANES), 1)
    in_half = [lane < HEAD_DIM, lane >= HEAD_DIM]
    kvh = hp // (H_A // KV_A // 2)
    kv_half = (lane >= kvh * HEAD_DIM) & (lane < (kvh + 1) * HEAD_DIM)
    sink = sink_ref[...]
    nblk = CHUNK // BLOCK
    scale = HEAD_DIM ** -0.5

    def body(i, carry):
        start = pl.multiple_of(i * BLOCK, BLOCK)
        bias = bias_buf[((ck > 0) | (i > 0)).astype(jnp.int32)]
        q = qa_ref[0, pl.ds(start, BLOCK), :] * scale
        kv = kva_buf[pl.ds(CHUNK - BLOCK + start, 2 * BLOCK), :]
        k = kv[:, 0:LANES]
        v = jnp.where(kv_half, kv[:, LANES:2 * LANES], 0.0)
        q_sw = pltpu.roll(q, HEAD_DIM, 1)
        acc = jnp.zeros((BLOCK, LANES), F32)
        for a in range(2):
            q_al = jnp.where(kvh == a, q, q_sw)
            qm = jnp.where(kv_half, q_al, 0.0)
            o, m, l = _half_softmax_pv(qm, k, v, bias)
            o = jnp.where(kvh == a, o, pltpu.roll(o, HEAD_DIM, 1))
            acc = acc + jnp.where(in_half[a], o / (l + jnp.exp(sink - m)), 0.0)
        oa_ref[0, pl.ds(start, BLOCK), :] = acc

        for bi, d in enumerate(DILATIONS):
            j, r = i // d, i % d
            qstart = j * (BLOCK * d) + r
            bias_b = bias_buf[((ck > 0) | (j > 0)).astype(jnp.int32)]
            if d == 1:
                qd = qb_ref[0, pl.ds(pl.multiple_of(qstart, BLOCK), BLOCK), :]
                kd = kb_buf[pl.ds(pl.multiple_of(CHUNK - BLOCK + qstart, BLOCK), 2 * BLOCK), :]
                vd = vb_buf[pl.ds(pl.multiple_of(CHUNK - BLOCK + qstart, BLOCK), 2 * BLOCK), :]
            else:
                qd = qb_ref[0, pl.ds(qstart, BLOCK, stride=d), :]
                kd = kb_buf[pl.ds(CHUNK - BLOCK * d + qstart, 2 * BLOCK, stride=d), :]
                vd = vb_buf[pl.ds(CHUNK - BLOCK * d + qstart, 2 * BLOCK, stride=d), :]
            o_acc = jnp.zeros((BLOCK, LANES), F32)
            lse_acc = jnp.zeros((BLOCK, LANES), F32)
            for a in range(2):
                qm = jnp.where(in_half[a], qd * scale, 0.0)
                vm = jnp.where(in_half[a], vd, 0.0)
                o, m, l = _half_softmax_pv(qm, kd, vm, bias_b)
                o_acc = o_acc + o / l
                lse_acc = lse_acc + jnp.where(in_half[a], m + jnp.log(l), 0.0)
            if d == 1:
                o_buf[bi, pl.ds(pl.multiple_of(qstart, BLOCK), BLOCK), :] = o_acc
                lse_buf[bi, pl.ds(pl.multiple_of(qstart, BLOCK), BLOCK), :] = lse_acc
            else:
                o_buf[bi, pl.ds(qstart, BLOCK, stride=d), :] = o_acc
                lse_buf[bi, pl.ds(qstart, BLOCK, stride=d), :] = lse_acc
        return carry

    lax.fori_loop(0, nblk, body, 0)

    l0, l1, l2 = lse_buf[0], lse_buf[1], lse_buf[2]
    mx = jnp.maximum(jnp.maximum(l0, l1), l2)
    w0, w1, w2 = jnp.exp(l0 - mx), jnp.exp(l1 - mx), jnp.exp(l2 - mx)
    tot = w0 + w1 + w2
    ob_ref[0] = (w0 / tot) * o_buf[0] + (w1 / tot) * o_buf[1] + (w2 / tot) * o_buf[2]


def _attn_prompt(qa, kva, qb, kvb, sink_row, batch, seq):
    qa = qa.reshape(batch, seq, Q_A)
    kva = kva.reshape(batch, seq, 2 * K_A)
    qb = qb.reshape(batch, seq, Q_B)
    kvb = kvb.reshape(batch, seq, 2 * Q_B)
    npair = Q_A // LANES
    cur = lambda b, h, c: (b, c, h)
    prev = lambda b, h, c: (b, jnp.maximum(c - 1, 0), h)
    blk = lambda width, fn: pl.BlockSpec((1, CHUNK, width), fn)
    oa, ob = pl.pallas_call(
        _attn_prompt_kernel,
        grid=(batch, npair, seq // CHUNK),
        in_specs=[
            blk(LANES, cur),
            blk(2 * K_A, lambda b, h, c: (b, c, 0)),
            blk(2 * K_A, lambda b, h, c: (b, jnp.maximum(c - 1, 0), 0)),
            blk(LANES, cur),
            blk(LANES, cur),
            blk(LANES, prev),
            blk(LANES, lambda b, h, c: (b, c, npair + h)),
            blk(LANES, lambda b, h, c: (b, jnp.maximum(c - 1, 0), npair + h)),
            pl.BlockSpec((1, LANES), lambda b, h, c: (0, h)),
        ],
        out_specs=[blk(LANES, cur), blk(LANES, cur)],
        out_shape=[jax.ShapeDtypeStruct((batch, seq, Q_A), F32), jax.ShapeDtypeStruct((batch, seq, Q_B), F32)],
        scratch_shapes=[
            pltpu.VMEM((2 * CHUNK, 2 * K_A), F32),
            pltpu.VMEM((2 * CHUNK, LANES), F32),
            pltpu.VMEM((2 * CHUNK, LANES), F32),
            pltpu.VMEM((len(DILATIONS), CHUNK, LANES), F32),
            pltpu.VMEM((len(DILATIONS), CHUNK, LANES), F32),
            pltpu.VMEM((2, BLOCK, 2 * BLOCK), F32),
        ],
        compiler_params=pltpu.CompilerParams(
            dimension_semantics=("parallel", "parallel", "arbitrary"), vmem_limit_bytes=VMEM_LIMIT),
        name="attn_prompt",
    )(qa, kva, kva, qb, kvb, kvb, kvb, kvb, sink_row)
    return oa.reshape(batch * seq, Q_A), ob.reshape(batch * seq, Q_B)


def _transpose_tail_kernel(x_ref, o_ref):
    o_ref[0] = x_ref[0].T


def _transpose_tail(x, rows):
    b, t, w = x.shape
    blk = min(rows, 512)
    first = (t - rows) // blk
    return pl.pallas_call(
        _transpose_tail_kernel,
        grid=(b, rows // blk),
        in_specs=[pl.BlockSpec((1, blk, w), lambda i, j: (i, first + j, 0))],
        out_specs=pl.BlockSpec((1, w, blk), lambda i, j: (i, 0, j)),
        out_shape=jax.ShapeDtypeStruct((b, w, rows), x.dtype),
        compiler_params=pltpu.CompilerParams(dimension_semantics=("parallel", "parallel")),
        name="transpose_tail",
    )(x)


def _sample_attn_kernel(sink_ref, swa_ref, dil_ref, qa_ref, kva_ref, qb_ref, kvb_ref,
                        swa_out_ref, dil_out_ref, oa_ref, ob_ref):
    n = pl.program_id(0)
    win = swa_ref.shape[-1]
    span = dil_ref.shape[-1]
    scale = HEAD_DIM ** -0.5

    @pl.when(n == 0)
    def _():
        oa_ref[...] = jnp.zeros_like(oa_ref)
        ob_ref[...] = jnp.zeros_like(ob_ref)

    mine = lax.broadcasted_iota(jnp.int32, (1, qa_ref.shape[1]), 1) == n

    def pick(ref, start):
        return jnp.sum(jnp.where(mine, ref[start:start + HEAD_DIM, :], 0.0), axis=1, keepdims=True)

    def put(ref, start, col):
        ref[start:start + HEAD_DIM, :] = jnp.where(mine, col, ref[start:start + HEAD_DIM, :])

    def attend(kt, vt, q, knew, vnew, weight, self_weight, extra):
        s = jnp.sum(kt * q, axis=0, keepdims=True) * scale
        s_self = jnp.sum(knew * q, axis=0, keepdims=True) * scale
        if weight is not None:
            s = jnp.where(weight > 0.0, s, NEG)
        m = jnp.maximum(jnp.max(s, axis=1, keepdims=True), s_self)
        p = jnp.exp(s - m)
        if weight is not None:
            p = p * weight
        p_self = self_weight * jnp.exp(s_self - m)
        l = jnp.sum(p, axis=1, keepdims=True) + p_self
        if extra is not None:
            l = l + jnp.exp(extra - m)
        return (jnp.sum(vt * p, axis=1, keepdims=True) + vnew * p_self) / l

    def shifted(x, new):
        width = x.shape[-1]
        rolled = pltpu.roll(x, width - 1, 1)
        last = lax.broadcasted_iota(jnp.int32, (1, width), 1) == width - 1
        return jnp.where(last, new, rolled)

    group = H_A // KV_A
    for hq in range(H_A):
        kvh = hq // group
        o = attend(swa_ref[0, 0, kvh], swa_ref[0, 1, kvh], pick(qa_ref, hq * HEAD_DIM),
                   pick(kva_ref, kvh * HEAD_DIM), pick(kva_ref, K_A + kvh * HEAD_DIM), None, 1.0, sink_ref[hq])
        put(oa_ref, hq * HEAD_DIM, o)
    for kv in range(2):
        for h in range(KV_A):
            swa_out_ref[0, kv, h] = shifted(swa_ref[0, kv, h], pick(kva_ref, kv * K_A + h * HEAD_DIM))

    pos = lax.broadcasted_iota(jnp.int32, (1, span), 1)
    weight = jnp.zeros((1, span), F32)
    for d in DILATIONS:
        weight = weight + jnp.where((pos % d == 0) & (pos >= span - BLOCK * d), 1.0, 0.0)
    for h in range(H_B):
        knew = pick(kvb_ref, h * HEAD_DIM)
        vnew = pick(kvb_ref, Q_B + h * HEAD_DIM)
        o = attend(dil_ref[0, 0, h], dil_ref[0, 1, h], pick(qb_ref, h * HEAD_DIM), knew, vnew, weight,
                   float(len(DILATIONS)), None)
        put(ob_ref, h * HEAD_DIM, o)
        dil_out_ref[0, 0, h] = shifted(dil_ref[0, 0, h], knew)
        dil_out_ref[0, 1, h] = shifted(dil_ref[0, 1, h], vnew)


def _sample_attn(sinks, swa, dil, qa_t, kva_t, qb_t, kvb_t):
    n = swa.shape[0]
    assert swa.shape[-1] == BLOCK and dil.shape[-1] == CHUNK
    per_sample = lambda a: pl.BlockSpec((1,) + a.shape[1:], lambda i: (i, 0, 0, 0, 0))
    full = lambda a: pl.BlockSpec(a.shape, lambda i: (0, 0))
    o_shape = jax.ShapeDtypeStruct((Q_A, n), F32)
    return pl.pallas_call(
        _sample_attn_kernel,
        grid=(n,),
        in_specs=[pl.BlockSpec(memory_space=pltpu.SMEM), per_sample(swa), per_sample(dil),
                  full(qa_t), full(kva_t), full(qb_t), full(kvb_t)],
        out_specs=[per_sample(swa), per_sample(dil), full(qa_t), full(qb_t)],
        out_shape=[jax.ShapeDtypeStruct(swa.shape, F32), jax.ShapeDtypeStruct(dil.shape, F32), o_shape, o_shape],
        compiler_params=pltpu.CompilerParams(dimension_semantics=("arbitrary",), vmem_limit_bytes=VMEM_LIMIT),
        name="sample_attn",
    )(sinks, swa, dil, qa_t, kva_t, qb_t, kvb_t)


def _mix_kernel(oa_ref, ob_ref, x_ref, ga_ref, gb_ref, wo_ref, gf_ref, wq_ref, keys_ref, xmid_ref, ht_ref, st_ref):
    o = jnp.concatenate([_rms(oa_ref[...], ga_ref[...]), _rms(ob_ref[...], gb_ref[...])], axis=-1)
    xm = x_ref[...] + jnp.dot(o.astype(BF16), wo_ref[...], preferred_element_type=F32)
    xmid_ref[...] = xm
    h = _rms(xm, gf_ref[...])
    hb = h.astype(BF16)
    ht_ref[...] = h.T.astype(BF16)
    q = jnp.dot(hb, wq_ref[...], preferred_element_type=F32).astype(BF16)
    half = PEER_QDIM // 2
    for i in range(2 * PEER_HEADS):
        st_ref[i] = lax.dot_general(keys_ref[i], q[:, i * half:(i + 1) * half], (((1,), (1,)), ((), ())),
                                    preferred_element_type=F32)


def _mix(oa, ob, x, ga, gb, wo, gf, wq, keys, tm):
    n = x.shape[0]
    nk = keys.shape[0]
    row = lambda width: pl.BlockSpec((tm, width), lambda i: (i, 0))
    full = lambda a: pl.BlockSpec(a.shape, lambda i: (0,) * a.ndim)
    return pl.pallas_call(
        _mix_kernel,
        grid=(n // tm,),
        in_specs=[row(Q_A), row(Q_B), row(D_MODEL), full(ga), full(gb), full(wo), full(gf), full(wq), full(keys)],
        out_specs=[row(D_MODEL), pl.BlockSpec((D_MODEL, tm), lambda i: (0, i)),
                   pl.BlockSpec((nk, N_KEYS, tm), lambda i: (0, 0, i))],
        out_shape=[jax.ShapeDtypeStruct((n, D_MODEL), F32), jax.ShapeDtypeStruct((D_MODEL, n), BF16),
                   jax.ShapeDtypeStruct((nk, N_KEYS, n), F32)],
        compiler_params=pltpu.CompilerParams(dimension_semantics=("parallel",), vmem_limit_bytes=VMEM_LIMIT),
        name="mix_out",
    )(oa, ob, x, ga, gb, wo, gf, wq, keys)


_CAND = [(i, j) for i in range(PEER_TOPK) for j in range(PEER_TOPK) if (i + 1) * (j + 1) <= PEER_TOPK]


def _merge_exchange_network(n):
    pairs = []
    p = 1
    while p < n:
        k = p
        while k >= 1:
            for j in range(k % p, n - k, 2 * k):
                for i in range(min(k, n - j - k)):
                    if (i + j) // (2 * p) == (i + j + k) // (2 * p):
                        pairs.append((i + j, i + j + k))
            k //= 2
        p *= 2
    return pairs


_SORT_NET = _merge_exchange_network(N_KEYS // SUBLANES)


def _topk_kernel(s_ref, g1_ref, g2_ref, thr_ref, a_scr, b_scr):
    def per_head(h, carry):
        for p, (g_ref, scr) in enumerate(((g1_ref, a_scr), (g2_ref, b_scr))):
            s = s_ref[2 * h + p]
            e = jnp.exp(s - jnp.max(s, axis=0, keepdims=True))
            g_ref[h] = e
            rows = [e[i * SUBLANES:(i + 1) * SUBLANES, :] for i in range(N_KEYS // SUBLANES)]
            for i, j in _SORT_NET:
                rows[i], rows[j] = jnp.maximum(rows[i], rows[j]), jnp.minimum(rows[i], rows[j])
            for k in range(PEER_TOPK):
                mk = jnp.max(rows[0], axis=0, keepdims=True)
                scr[k, pl.ds(h, 1), :] = mk
                popped = rows[0] == mk
                for d in range(PEER_TOPK - 1 - k):
                    rows[d] = jnp.where(popped, rows[d + 1], rows[d])
        return carry

    lax.fori_loop(0, PEER_HEADS, per_head, 0)

    a = [a_scr[k] for k in range(PEER_TOPK)]
    b = [b_scr[k] for k in range(PEER_TOPK)]
    cand = {(i, j): a[i] * b[j] for i, j in _CAND}
    lists = [[cand[(i, j)] for j in range(PEER_TOPK // (i + 1))] for i in range(PEER_TOPK)]
    z = jnp.zeros_like(a[0])
    theta = z
    for k in range(PEER_TOPK):
        mk = functools.reduce(jnp.maximum, [lst[0] for lst in lists])
        z = z + jnp.maximum(mk, 0.0)
        theta = jnp.where(mk > 0.0, mk, theta)
        for lst in lists:
            popped = lst[0] == mk
            keep = min(len(lst), PEER_TOPK - k)
            for d in range(keep):
                lst[d] = jnp.where(popped, lst[d + 1] if d + 1 < len(lst) else -1.0, lst[d])
    rz = 0.5 / z
    thr = jnp.full_like(z, 2.0)
    for (i, j), c in cand.items():
        thr = jnp.minimum(thr, jnp.where(c >= theta, (a[i] * rz) * b[j], 2.0))
    thr_ref[...] = thr
    for h in range(PEER_HEADS):
        g1_ref[h] = g1_ref[h] * rz[h:h + 1, :]


def _peer_topk(st, tb):
    nk, _, n = st.shape
    heads = nk // 2
    g_spec = pl.BlockSpec((heads, N_KEYS, tb), lambda i: (0, 0, i))
    g_shape = jax.ShapeDtypeStruct((heads, N_KEYS, n), F32)
    return pl.pallas_call(
        _topk_kernel,
        grid=(n // tb,),
        in_specs=[pl.BlockSpec((nk, N_KEYS, tb), lambda i: (0, 0, i))],
        out_specs=[g_spec, g_spec, pl.BlockSpec((heads, tb), lambda i: (0, i))],
        out_shape=[g_shape, g_shape, jax.ShapeDtypeStruct((heads, n), F32)],
        scratch_shapes=[pltpu.VMEM((PEER_TOPK, heads, tb), F32), pltpu.VMEM((PEER_TOPK, heads, tb), F32)],
        compiler_params=pltpu.CompilerParams(dimension_semantics=("parallel",), vmem_limit_bytes=VMEM_LIMIT),
        name="peer_topk",
    )(st)


def _transpose_cast_kernel(x_ref, o_ref):
    o_ref[...] = x_ref[...].T.astype(o_ref.dtype)


def _transpose_cast(x, dtype, blk):
    r, c = x.shape
    return pl.pallas_call(
        _transpose_cast_kernel,
        grid=(r // blk, c // blk),
        in_specs=[pl.BlockSpec((blk, blk), lambda i, j: (i, j))],
        out_specs=pl.BlockSpec((blk, blk), lambda i, j: (j, i)),
        out_shape=jax.ShapeDtypeStruct((c, r), dtype),
        compiler_params=pltpu.CompilerParams(dimension_semantics=("parallel", "parallel")),
        name="transpose_cast",
    )(x)


_GELU_C0 = 0.7978845608028654
_GELU_C1 = _GELU_C0 * 0.044715


def _peer_kernel(u_ref, vt_ref, ht_ref, g1_ref, g2_ref, thr_ref, xmid_ref, gfin_ref, y_ref,
                 acc, act0, act1, gate0, gate1):
    e = pl.program_id(1)
    n_i1 = u_ref.shape[0] // N_KEYS
    tt = act0.shape[1]
    nt = ht_ref.shape[1] // tt

    @pl.when(e == 0)
    def _():
        acc[...] = jnp.zeros_like(acc)

    def tile(t):
        return slice(t * tt, (t + 1) * tt)

    def act_matmul(buf, t):
        buf[...] = jnp.dot(u_ref[...], ht_ref[:, tile(t)], preferred_element_type=F32)

    def out_matmul(buf, t):
        acc[:, tile(t)] += jnp.dot(vt_ref[...], buf[...], preferred_element_type=F32)

    def gating(act_buf, gate_buf, t):
        for sub in range(tt // LANES):
            tok = slice(t * tt + sub * LANES, t * tt + (sub + 1) * LANES)
            cols = slice(sub * LANES, (sub + 1) * LANES)
            for il in range(n_i1):
                w = jnp.zeros((N_KEYS, LANES), F32)
                for h in range(PEER_HEADS):
                    prod = g1_ref[h, il:il + 1, tok] * g2_ref[h, :, tok]
                    w = w + jnp.where(prod >= thr_ref[h:h + 1, tok], prod, 0.0)
                rows = slice(il * N_KEYS, (il + 1) * N_KEYS)
                x = act_buf[rows, cols]
                t_ = jnp.tanh(x * (_GELU_C0 + _GELU_C1 * (x * x)))
                gate_buf[rows, cols] = ((w * x) * (1.0 + t_)).astype(BF16)

    acts, gates = (act0, act1), (gate0, gate1)
    act_matmul(acts[0], 0)
    for t in range(nt):
        if t + 1 < nt:
            act_matmul(acts[(t + 1) % 2], t + 1)
        gating(acts[t % 2], gates[t % 2], t)
        if t >= 1:
            out_matmul(gates[(t - 1) % 2], t - 1)
    out_matmul(gates[(nt - 1) % 2], nt - 1)

    @pl.when(e == pl.num_programs(1) - 1)
    def _():
        y_ref[...] = _rms(xmid_ref[...] + acc[...].T, gfin_ref[...])


def _peer(u_bf16, vt_bf16, ht, g1, g2, thr, xmid, gfin, tb, eb):
    n = xmid.shape[0]
    heads = g1.shape[0]
    tt = min(tb, 2 * LANES)
    return pl.pallas_call(
        _peer_kernel,
        grid=(n // tb, N_EXPERTS // eb),
        in_specs=[
            pl.BlockSpec((eb, D_MODEL), lambda t, e: (e, 0)),
            pl.BlockSpec((D_MODEL, eb), lambda t, e: (0, e)),
            pl.BlockSpec((D_MODEL, tb), lambda t, e: (0, t)),
            pl.BlockSpec((heads, eb // N_KEYS, tb), lambda t, e: (0, e, t)),
            pl.BlockSpec((heads, N_KEYS, tb), lambda t, e: (0, 0, t)),
            pl.BlockSpec((heads, tb), lambda t, e: (0, t)),
            pl.BlockSpec((tb, D_MODEL), lambda t, e: (t, 0)),
            pl.BlockSpec((1, D_MODEL), lambda t, e: (0, 0)),
        ],
        out_specs=pl.BlockSpec((tb, D_MODEL), lambda t, e: (t, 0)),
        out_shape=jax.ShapeDtypeStruct((n, D_MODEL), F32),
        scratch_shapes=[pltpu.VMEM((D_MODEL, tb), F32), pltpu.VMEM((eb, tt), F32), pltpu.VMEM((eb, tt), F32),
                        pltpu.VMEM((eb, tt), BF16), pltpu.VMEM((eb, tt), BF16)],
        compiler_params=pltpu.CompilerParams(
            dimension_semantics=("parallel", "arbitrary"), vmem_limit_bytes=VMEM_LIMIT),
        name="peer_dense",
    )(u_bf16, vt_bf16, ht, g1, g2, thr, xmid, gfin)


def _ffn_and_norm(oa, ob, x, params, tm, tb_topk, tb, eb):
    xmid, ht, st = _mix(oa, ob, x, params["ga"], params["gb"], params["wo"], params["gf"], params["wq"],
                        params["keys"], tm)
    g1, g2, thr = _peer_topk(st, tb_topk)
    return _peer(params["u"], params["vt"], ht, g1, g2, thr, xmid, params["gfin"], tb, eb)


def kernel(x_prompt, x_sample, cache_swa_kv, cache_dil_kv, attn_norm, w_in, sinks, out_norm_a, out_norm_b, w_out,
           ffn_norm, peer_w_q, peer_sub_keys, peer_u, peer_v, final_norm):
    depth = w_in.shape[0]
    assert depth == 1
    batch, seq, _ = x_prompt.shape
    n_s, s_len, _ = x_sample.shape
    assert s_len == 1 and seq % CHUNK == 0
    win_a = min(BLOCK, seq)
    win_b = min(CHUNK, seq)

    w_in_b = w_in[0].astype(BF16)
    g_attn = attn_norm[0][None]
    params = dict(
        ga=out_norm_a[0][None], gb=out_norm_b[0][None], wo=w_out[0].astype(BF16), gf=ffn_norm[0][None],
        wq=peer_w_q[0].astype(BF16),
        keys=peer_sub_keys[0].reshape(2 * PEER_HEADS, N_KEYS, PEER_QDIM // 2).astype(BF16),
        u=peer_u[0].astype(BF16), vt=_transpose_cast(peer_v[0], BF16, 1024), gfin=final_norm[None])
    sink_row = jnp.repeat(sinks[0].astype(F32), HEAD_DIM)[None]
    to_rows = lambda a: jnp.transpose(a, (0, 4, 1, 2, 3))[None]
    to_lanes = lambda a: jnp.transpose(a[0], (0, 2, 3, 4, 1))

    xp = x_prompt.reshape(batch * seq, D_MODEL)
    pos_p = jnp.tile(jnp.arange(seq), batch)
    qa, kva, qb, kvb = _project(xp, pos_p, g_attn, w_in_b, 512)
    oa, ob = _attn_prompt(qa, kva, qb, kvb, sink_row, batch, seq)
    y_prompt = _ffn_and_norm(oa, ob, xp, params, 256, 128, 1024, 1024).reshape(batch, seq, D_MODEL)
    swa_p = to_rows(_transpose_tail(kva.reshape(batch, seq, 2 * K_A), win_a).reshape(batch, 2, KV_A, HEAD_DIM, win_a))
    dil_p = to_rows(_transpose_tail(kvb.reshape(batch, seq, 2 * Q_B), win_b).reshape(batch, 2, H_B, HEAD_DIM, win_b))

    xs = x_sample.reshape(n_s, D_MODEL)
    pos_s = jnp.full((n_s,), PAST_LEN, jnp.int32)
    qa_s, kva_s, qb_s, kvb_s = _project(xs, pos_s, g_attn, w_in_b, n_s)
    swa_new, dil_new, oa_t, ob_t = _sample_attn(sinks[0].astype(F32), to_lanes(cache_swa_kv), to_lanes(cache_dil_kv),
                                                qa_s.T, kva_s.T, qb_s.T, kvb_s.T)
    y_sample = _ffn_and_norm(oa_t.T, ob_t.T, xs, params, n_s, 128, 128, 1024).reshape(n_s, 1, D_MODEL)
    return y_prompt, y_sample, swa_p, dil_p, to_rows(swa_new), to_rows(dil_new)
```

```python
import functools

import jax
import jax.numpy as jnp
from jax import lax
from jax.experimental import pallas as pl
from jax.experimental.pallas import tpu as pltpu

F32 = jnp.float32
BF16 = jnp.bfloat16

D_MODEL = 1024
HEAD_DIM = 64
PAST_LEN = 8192
H_A = 8
KV_A = 2
H_B = 8
Q_A = H_A * HEAD_DIM
K_A = KV_A * HEAD_DIM
Q_B = H_B * HEAD_DIM
D_IN = Q_A + 2 * K_A + 3 * Q_B
ROT_DIM = HEAD_DIM // 4
ROPE_THETA = 500000.0
BLOCK = 128
DILATIONS = (1, 4, 16)
CHUNK = BLOCK * DILATIONS[-1]
EPS = 1e-6
LANES = 128
SUBLANES = 8
NEG = -1e30

N_KEYS = 128
PEER_HEADS = 8
PEER_TOPK = 16
PEER_QDIM = 256
N_EXPERTS = N_KEYS * N_KEYS

VMEM_LIMIT = 56 * 1024 * 1024


def _rms(x, g):
    return (x * lax.rsqrt(jnp.mean(x * x, axis=-1, keepdims=True) + EPS)) * g


def _rope_tables(pos):
    inv = ROPE_THETA ** (-jnp.arange(0, ROT_DIM, 2, dtype=F32) / ROT_DIM)
    ang = pos.astype(F32)[:, None] * inv[None, :]
    cos, sin = jnp.cos(ang), jnp.sin(ang)
    t = pos.shape[0]
    half = ROT_DIM // 2
    rest = HEAD_DIM - ROT_DIM
    c = jnp.concatenate([cos, cos, jnp.ones((t, rest), F32)], axis=-1)
    s1 = jnp.concatenate([-sin, jnp.zeros((t, half + rest), F32)], axis=-1)
    s2 = jnp.concatenate([jnp.zeros((t, half), F32), sin, jnp.zeros((t, rest), F32)], axis=-1)
    rep = LANES // HEAD_DIM
    return jnp.tile(c, (1, rep)), jnp.tile(s1, (1, rep)), jnp.tile(s2, (1, rep))


def _proj_kernel(x_ref, g_ref, w_ref, c_ref, s1_ref, s2_ref, qa_ref, kva_ref, qb_ref, kvb_ref):
    h = _rms(x_ref[...], g_ref[...])
    z = jnp.dot(h.astype(BF16), w_ref[...], preferred_element_type=F32)
    c, s1, s2 = c_ref[...], s1_ref[...], s2_ref[...]
    half = ROT_DIM // 2

    def chunk(i, rot):
        zc = z[:, i * LANES:(i + 1) * LANES]
        if not rot:
            return zc
        return zc * c + pltpu.roll(zc, LANES - half, 1) * s1 + pltpu.roll(zc, half, 1) * s2

    nq = Q_A // LANES
    for i in range(nq):
        qa_ref[:, i * LANES:(i + 1) * LANES] = chunk(i, True)
    kva_ref[:, 0:LANES] = chunk(nq, True)
    kva_ref[:, LANES:2 * LANES] = chunk(nq + 1, False)
    base = nq + 2
    for i in range(nq):
        qb_ref[:, i * LANES:(i + 1) * LANES] = chunk(base + i, True)
    for i in range(nq):
        kvb_ref[:, i * LANES:(i + 1) * LANES] = chunk(base + nq + i, True)
    for i in range(nq):
        kvb_ref[:, Q_B + i * LANES:Q_B + (i + 1) * LANES] = chunk(base + 2 * nq + i, False)


def _project(x, pos, g, w_bf16, tm):
    n = x.shape[0]
    c, s1, s2 = _rope_tables(pos)
    per_seq = pos.shape[0] // tm
    row = lambda width: pl.BlockSpec((tm, width), lambda i: (i, 0))
    tab = pl.BlockSpec((tm, LANES), lambda i: (i % per_seq, 0))
    full = lambda a: pl.BlockSpec(a.shape, lambda i: (0, 0))
    return pl.pallas_call(
        _proj_kernel,
        grid=(n // tm,),
        in_specs=[row(D_MODEL), full(g), full(w_bf16), tab, tab, tab],
        out_specs=[row(Q_A), row(2 * K_A), row(Q_B), row(2 * Q_B)],
        out_shape=[jax.ShapeDtypeStruct((n, Q_A), F32), jax.ShapeDtypeStruct((n, 2 * K_A), F32),
                   jax.ShapeDtypeStruct((n, Q_B), F32), jax.ShapeDtypeStruct((n, 2 * Q_B), F32)],
        compiler_params=pltpu.CompilerParams(dimension_semantics=("parallel",), vmem_limit_bytes=VMEM_LIMIT),
        name="in_proj",
    )(x, g, w_bf16, c, s1, s2)


def _band_bias(with_prev):
    r = lax.broadcasted_iota(jnp.int32, (BLOCK, 2 * BLOCK), 0)
    k = lax.broadcasted_iota(jnp.int32, (BLOCK, 2 * BLOCK), 1)
    dist = r + BLOCK - k
    ok = (dist >= 0) & (dist <= BLOCK)
    if not with_prev:
        ok = ok & (k >= BLOCK)
    return jnp.where(ok, 0.0, NEG)


def _half_softmax_pv(qm, k, vm, bias):
    s = lax.dot_general(qm.astype(BF16), k.astype(BF16), (((1,), (1,)), ((), ())), preferred_element_type=F32) + bias
    m = jnp.max(s, axis=-1, keepdims=True)
    p = jnp.exp(s - m)
    l = jnp.sum(p, axis=-1, keepdims=True)
    o = jnp.dot(p.astype(BF16), vm.astype(BF16), preferred_element_type=F32)
    return o, m, l


def _attn_prompt_kernel(qa_ref, kva_c_ref, kva_p_ref, qb_ref, kb_c_ref, kb_p_ref, vb_c_ref, vb_p_ref, sink_ref,
                        oa_ref, ob_ref, kva_buf, kb_buf, vb_buf, o_buf, lse_buf, bias_buf):
    hp = pl.program_id(1)
    ck = pl.program_id(2)
    bias_buf[0] = _band_bias(False)
    bias_buf[1] = _band_bias(True)
    kva_buf[0:CHUNK] = kva_p_ref[0]
    kva_buf[CHUNK:2 * CHUNK] = kva_c_ref[0]
    kb_buf[0:CHUNK] = kb_p_ref[0]
    kb_buf[CHUNK:2 * CHUNK] = kb_c_ref[0]
    vb_buf[0:CHUNK] = vb_p_ref[0]
    vb_buf[CHUNK:2 * CHUNK] = vb_c_ref[0]

    lane = lax.broadcasted_iota(jnp.int32, (1, LANES), 1)
    in_half = [lane < HEAD_DIM, lane >= HEAD_DIM]
    kvh = hp // (H_A // KV_A // 2)
    kv_half = (lane >= kvh * HEAD_DIM) & (lane < (kvh + 1) * HEAD_DIM)
    sink = sink_ref[...]
    nblk = CHUNK // BLOCK
    scale = HEAD_DIM ** -0.5

    def body(i, carry):
        start = pl.multiple_of(i * BLOCK, BLOCK)
        bias = bias_buf[((ck > 0) | (i > 0)).astype(jnp.int32)]
        q = qa_ref[0, pl.ds(start, BLOCK), :] * scale
        kv = kva_buf[pl.ds(CHUNK - BLOCK + start, 2 * BLOCK), :]
        k = kv[:, 0:LANES]
        v = jnp.where(kv_half, kv[:, LANES:2 * LANES], 0.0)
        q_sw = pltpu.roll(q, HEAD_DIM, 1)
        acc = jnp.zeros((BLOCK, LANES), F32)
        for a in range(2):
            q_al = jnp.where(kvh == a, q, q_sw)
            qm = jnp.where(kv_half, q_al, 0.0)
            o, m, l = _half_softmax_pv(qm, k, v, bias)
            o = jnp.where(kvh == a, o, pltpu.roll(o, HEAD_DIM, 1))
            acc = acc + jnp.where(in_half[a], o / (l + jnp.exp(sink - m)), 0.0)
        oa_ref[0, pl.ds(start, BLOCK), :] = acc

        for bi, d in enumerate(DILATIONS):
            j, r = i // d, i % d
            qstart = j * (BLOCK * d) + r
            bias_b = bias_buf[((ck > 0) | (j > 0)).astype(jnp.int32)]
            if d == 1:
                qd = qb_ref[0, pl.ds(pl.multiple_of(qstart, BLOCK), BLOCK), :]
                kd = kb_buf[pl.ds(pl.multiple_of(CHUNK - BLOCK + qstart, BLOCK), 2 * BLOCK), :]
                vd = vb_buf[pl.ds(pl.multiple_of(CHUNK - BLOCK + qstart, BLOCK), 2 * BLOCK), :]
            else:
                qd = qb_ref[0, pl.ds(qstart, BLOCK, stride=d), :]
                kd = kb_buf[pl.ds(CHUNK - BLOCK * d + qstart, 2 * BLOCK, stride=d), :]
                vd = vb_buf[pl.ds(CHUNK - BLOCK * d + qstart, 2 * BLOCK, stride=d), :]
            o_acc = jnp.zeros((BLOCK, LANES), F32)
            lse_acc = jnp.zeros((BLOCK, LANES), F32)
            for a in range(2):
                qm = jnp.where(in_half[a], qd * scale, 0.0)
                vm = jnp.where(in_half[a], vd, 0.0)
                o, m, l = _half_softmax_pv(qm, kd, vm, bias_b)
                o_acc = o_acc + o / l
                lse_acc = lse_acc + jnp.where(in_half[a], m + jnp.log(l), 0.0)
            if d == 1:
                o_buf[bi, pl.ds(pl.multiple_of(qstart, BLOCK), BLOCK), :] = o_acc
                lse_buf[bi, pl.ds(pl.multiple_of(qstart, BLOCK), BLOCK), :] = lse_acc
            else:
                o_buf[bi, pl.ds(qstart, BLOCK, stride=d), :] = o_acc
                lse_buf[bi, pl.ds(qstart, BLOCK, stride=d), :] = lse_acc
        return carry

    lax.fori_loop(0, nblk, body, 0, unroll=2)

    l0, l1, l2 = lse_buf[0], lse_buf[1], lse_buf[2]
    mx = jnp.maximum(jnp.maximum(l0, l1), l2)
    w0, w1, w2 = jnp.exp(l0 - mx), jnp.exp(l1 - mx), jnp.exp(l2 - mx)
    tot = w0 + w1 + w2
    ob_ref[0] = (w0 / tot) * o_buf[0] + (w1 / tot) * o_buf[1] + (w2 / tot) * o_buf[2]


def _attn_prompt(qa, kva, qb, kvb, sink_row, batch, seq):
    qa = qa.reshape(batch, seq, Q_A)
    kva = kva.reshape(batch, seq, 2 * K_A)
    qb = qb.reshape(batch, seq, Q_B)
    kvb = kvb.reshape(batch, seq, 2 * Q_B)
    npair = Q_A // LANES
    cur = lambda b, h, c: (b, c, h)
    prev = lambda b, h, c: (b, jnp.maximum(c - 1, 0), h)
    blk = lambda width, fn: pl.BlockSpec((1, CHUNK, width), fn)
    oa, ob = pl.pallas_call(
        _attn_prompt_kernel,
        grid=(batch, npair, seq // CHUNK),
        in_specs=[
            blk(LANES, cur),
            blk(2 * K_A, lambda b, h, c: (b, c, 0)),
            blk(2 * K_A, lambda b, h, c: (b, jnp.maximum(c - 1, 0), 0)),
            blk(LANES, cur),
            blk(LANES, cur),
            blk(LANES, prev),
            blk(LANES, lambda b, h, c: (b, c, npair + h)),
            blk(LANES, lambda b, h, c: (b, jnp.maximum(c - 1, 0), npair + h)),
            pl.BlockSpec((1, LANES), lambda b, h, c: (0, h)),
        ],
        out_specs=[blk(LANES, cur), blk(LANES, cur)],
        out_shape=[jax.ShapeDtypeStruct((batch, seq, Q_A), F32), jax.ShapeDtypeStruct((batch, seq, Q_B), F32)],
        scratch_shapes=[
            pltpu.VMEM((2 * CHUNK, 2 * K_A), F32),
            pltpu.VMEM((2 * CHUNK, LANES), F32),
            pltpu.VMEM((2 * CHUNK, LANES), F32),
            pltpu.VMEM((len(DILATIONS), CHUNK, LANES), F32),
            pltpu.VMEM((len(DILATIONS), CHUNK, LANES), F32),
            pltpu.VMEM((2, BLOCK, 2 * BLOCK), F32),
        ],
        compiler_params=pltpu.CompilerParams(
            dimension_semantics=("parallel", "parallel", "arbitrary"), vmem_limit_bytes=VMEM_LIMIT),
        name="attn_prompt",
    )(qa, kva, kva, qb, kvb, kvb, kvb, kvb, sink_row)
    return oa.reshape(batch * seq, Q_A), ob.reshape(batch * seq, Q_B)


def _transpose_tail_kernel(x_ref, o_ref):
    o_ref[0] = x_ref[0].T


def _transpose_tail(x, rows):
    b, t, w = x.shape
    blk = min(rows, 512)
    first = (t - rows) // blk
    return pl.pallas_call(
        _transpose_tail_kernel,
        grid=(b, rows // blk),
        in_specs=[pl.BlockSpec((1, blk, w), lambda i, j: (i, first + j, 0))],
        out_specs=pl.BlockSpec((1, w, blk), lambda i, j: (i, 0, j)),
        out_shape=jax.ShapeDtypeStruct((b, w, rows), x.dtype),
        compiler_params=pltpu.CompilerParams(dimension_semantics=("parallel", "parallel")),
        name="transpose_tail",
    )(x)


def _sample_attn_kernel(sink_ref, swa_ref, dil_ref, qa_ref, kva_ref, qb_ref, kvb_ref,
                        swa_out_ref, dil_out_ref, oa_ref, ob_ref, sa_scr, selfa_scr, sb_scr, selfb_scr):
    n = pl.program_id(0)
    win = swa_ref.shape[-1]
    span = dil_ref.shape[-1]
    scale = HEAD_DIM ** -0.5

    @pl.when(n == 0)
    def _():
        oa_ref[...] = jnp.zeros_like(oa_ref)
        ob_ref[...] = jnp.zeros_like(ob_ref)

    mine = lax.broadcasted_iota(jnp.int32, (1, qa_ref.shape[1]), 1) == n

    picked = {}

    def pick(ref, start):
        if (id(ref), start) not in picked:
            picked[(id(ref), start)] = jnp.sum(jnp.where(mine, ref[start:start + HEAD_DIM, :], 0.0), axis=1,
                                               keepdims=True)
        return picked[(id(ref), start)]

    def put(ref, start, col):
        ref[start:start + HEAD_DIM, :] = jnp.where(mine, col, ref[start:start + HEAD_DIM, :])

    def attend(s_scr, self_scr, k_of, v_of, q_of, knew_of, vnew_of, weight, self_weight, extra, o_ref):
        for h in range(H_A):
            q = q_of(h)
            s_scr[h:h + 1, :] = jnp.sum(k_of(h) * q, axis=0, keepdims=True)
            self_scr[h:h + 1, :] = jnp.broadcast_to(jnp.sum(knew_of(h) * q, axis=0, keepdims=True), (1, LANES))
        s = s_scr[...] * scale
        s_self = self_scr[:, 0:1] * scale
        if weight is not None:
            s = jnp.where(weight > 0.0, s, NEG)
        m = jnp.maximum(jnp.max(s, axis=1, keepdims=True), s_self)
        p = jnp.exp(s - m)
        if weight is not None:
            p = p * weight
        p_self = self_weight * jnp.exp(s_self - m)
        l = jnp.sum(p, axis=1, keepdims=True) + p_self
        if extra is not None:
            l = l + jnp.exp(extra - m)
        s_scr[...] = p / l
        self_scr[...] = jnp.broadcast_to(p_self / l, self_scr.shape)
        for h in range(H_A):
            o = jnp.sum(v_of(h) * s_scr[h:h + 1, :], axis=1, keepdims=True) + vnew_of(h) * self_scr[h:h + 1, 0:1]
            put(o_ref, h * HEAD_DIM, o)

    def shifted(x, new):
        width = x.shape[-1]
        rolled = pltpu.roll(x, width - 1, 1)
        last = lax.broadcasted_iota(jnp.int32, (1, width), 1) == width - 1
        return jnp.where(last, new, rolled)

    group = H_A // KV_A
    head_id = lax.broadcasted_iota(jnp.int32, (H_A, 1), 0)
    sink = jnp.zeros((H_A, 1), F32)
    for hq in range(H_A):
        sink = jnp.where(head_id == hq, sink_ref[hq], sink)
    attend(sa_scr, selfa_scr,
           lambda h: swa_ref[0, 0, h // group], lambda h: swa_ref[0, 1, h // group],
           lambda h: pick(qa_ref, h * HEAD_DIM),
           lambda h: pick(kva_ref, (h // group) * HEAD_DIM), lambda h: pick(kva_ref, K_A + (h // group) * HEAD_DIM),
           None, 1.0, sink, oa_ref)
    for kv in range(2):
        for h in range(KV_A):
            swa_out_ref[0, kv, h] = shifted(swa_ref[0, kv, h], pick(kva_ref, kv * K_A + h * HEAD_DIM))

    pos = lax.broadcasted_iota(jnp.int32, (1, span), 1)
    weight = jnp.zeros((1, span), F32)
    for d in DILATIONS:
        weight = weight + jnp.where((pos % d == 0) & (pos >= span - BLOCK * d), 1.0, 0.0)
    attend(sb_scr, selfb_scr,
           lambda h: dil_ref[0, 0, h], lambda h: dil_ref[0, 1, h], lambda h: pick(qb_ref, h * HEAD_DIM),
           lambda h: pick(kvb_ref, h * HEAD_DIM), lambda h: pick(kvb_ref, Q_B + h * HEAD_DIM),
           weight, float(len(DILATIONS)), None, ob_ref)
    for h in range(H_B):
        dil_out_ref[0, 0, h] = shifted(dil_ref[0, 0, h], pick(kvb_ref, h * HEAD_DIM))
        dil_out_ref[0, 1, h] = shifted(dil_ref[0, 1, h], pick(kvb_ref, Q_B + h * HEAD_DIM))


def _sample_attn(sinks, swa, dil, qa_t, kva_t, qb_t, kvb_t):
    n = swa.shape[0]
    assert swa.shape[-1] == BLOCK and dil.shape[-1] == CHUNK
    per_sample = lambda a: pl.BlockSpec((1,) + a.shape[1:], lambda i: (i, 0, 0, 0, 0))
    full = lambda a: pl.BlockSpec(a.shape, lambda i: (0, 0))
    o_shape = jax.ShapeDtypeStruct((Q_A, n), F32)
    return pl.pallas_call(
        _sample_attn_kernel,
        grid=(n,),
        in_specs=[pl.BlockSpec(memory_space=pltpu.SMEM), per_sample(swa), per_sample(dil),
                  full(qa_t), full(kva_t), full(qb_t), full(kvb_t)],
        out_specs=[per_sample(swa), per_sample(dil), full(qa_t), full(qb_t)],
        out_shape=[jax.ShapeDtypeStruct(swa.shape, F32), jax.ShapeDtypeStruct(dil.shape, F32), o_shape, o_shape],
        scratch_shapes=[pltpu.VMEM((H_A, BLOCK), F32), pltpu.VMEM((H_A, LANES), F32),
                        pltpu.VMEM((H_B, CHUNK), F32), pltpu.VMEM((H_B, LANES), F32)],
        compiler_params=pltpu.CompilerParams(dimension_semantics=("arbitrary",), vmem_limit_bytes=VMEM_LIMIT),
        name="sample_attn",
    )(sinks, swa, dil, qa_t, kva_t, qb_t, kvb_t)


def _mix_kernel(oa_ref, ob_ref, x_ref, ga_ref, gb_ref, wo_ref, gf_ref, wq_ref, keys_ref, xmid_ref, ht_ref, st_ref):
    o = jnp.concatenate([_rms(oa_ref[...], ga_ref[...]), _rms(ob_ref[...], gb_ref[...])], axis=-1)
    xm = x_ref[...] + jnp.dot(o.astype(BF16), wo_ref[...], preferred_element_type=F32)
    xmid_ref[...] = xm
    h = _rms(xm, gf_ref[...])
    hb = h.astype(BF16)
    ht_ref[...] = h.T.astype(BF16)
    q = jnp.dot(hb, wq_ref[...], preferred_element_type=F32).astype(BF16)
    half = PEER_QDIM // 2
    for i in range(2 * PEER_HEADS):
        st_ref[i] = lax.dot_general(keys_ref[i], q[:, i * half:(i + 1) * half], (((1,), (1,)), ((), ())),
                                    preferred_element_type=F32)


def _mix(oa, ob, x, ga, gb, wo, gf, wq, keys, tm):
    n = x.shape[0]
    nk = keys.shape[0]
    row = lambda width: pl.BlockSpec((tm, width), lambda i: (i, 0))
    full = lambda a: pl.BlockSpec(a.shape, lambda i: (0,) * a.ndim)
    return pl.pallas_call(
        _mix_kernel,
        grid=(n // tm,),
        in_specs=[row(Q_A), row(Q_B), row(D_MODEL), full(ga), full(gb), full(wo), full(gf), full(wq), full(keys)],
        out_specs=[row(D_MODEL), pl.BlockSpec((D_MODEL, tm), lambda i: (0, i)),
                   pl.BlockSpec((nk, N_KEYS, tm), lambda i: (0, 0, i))],
        out_shape=[jax.ShapeDtypeStruct((n, D_MODEL), F32), jax.ShapeDtypeStruct((D_MODEL, n), BF16),
                   jax.ShapeDtypeStruct((nk, N_KEYS, n), F32)],
        compiler_params=pltpu.CompilerParams(dimension_semantics=("parallel",), vmem_limit_bytes=VMEM_LIMIT),
        name="mix_out",
    )(oa, ob, x, ga, gb, wo, gf, wq, keys)


_CAND = [(i, j) for i in range(PEER_TOPK) for j in range(PEER_TOPK) if (i + 1) * (j + 1) <= PEER_TOPK]


def _merge_exchange_network(n):
    pairs = []
    p = 1
    while p < n:
        k = p
        while k >= 1:
            for j in range(k % p, n - k, 2 * k):
                for i in range(min(k, n - j - k)):
                    if (i + j) // (2 * p) == (i + j + k) // (2 * p):
                        pairs.append((i + j, i + j + k))
            k //= 2
        p *= 2
    return pairs


_SORT_NET = _merge_exchange_network(N_KEYS // SUBLANES)


def _topk_kernel(s_ref, g1_ref, g2_ref, thr_ref, a_scr, b_scr):
    def per_head(h, carry):
        for p, (g_ref, scr) in enumerate(((g1_ref, a_scr), (g2_ref, b_scr))):
            s = s_ref[2 * h + p]
            e = jnp.exp(s - jnp.max(s, axis=0, keepdims=True))
            g_ref[h] = e
            rows = [e[i * SUBLANES:(i + 1) * SUBLANES, :] for i in range(N_KEYS // SUBLANES)]
            for i, j in _SORT_NET:
                rows[i], rows[j] = jnp.maximum(rows[i], rows[j]), jnp.minimum(rows[i], rows[j])
            for k in range(PEER_TOPK):
                mk = jnp.max(rows[0], axis=0, keepdims=True)
                scr[k, pl.ds(h, 1), :] = mk
                popped = rows[0] == mk
                for d in range(PEER_TOPK - 1 - k):
                    rows[d] = jnp.where(popped, rows[d + 1], rows[d])
        return carry

    lax.fori_loop(0, PEER_HEADS, per_head, 0)

    a = [a_scr[k] for k in range(PEER_TOPK)]
    b = [b_scr[k] for k in range(PEER_TOPK)]
    cand = {(i, j): a[i] * b[j] for i, j in _CAND}
    lists = [[cand[(i, j)] for j in range(PEER_TOPK // (i + 1))] for i in range(PEER_TOPK)]
    z = jnp.zeros_like(a[0])
    theta = z
    for k in range(PEER_TOPK):
        mk = functools.reduce(jnp.maximum, [lst[0] for lst in lists])
        z = z + jnp.maximum(mk, 0.0)
        theta = jnp.where(mk > 0.0, mk, theta)
        for lst in lists:
            popped = lst[0] == mk
            keep = min(len(lst), PEER_TOPK - k)
            for d in range(keep):
                lst[d] = jnp.where(popped, lst[d + 1] if d + 1 < len(lst) else -1.0, lst[d])
    rz = 0.5 / z
    thr = jnp.full_like(z, 2.0)
    for (i, j), c in cand.items():
        thr = jnp.minimum(thr, jnp.where(c >= theta, (a[i] * rz) * b[j], 2.0))
    thr_ref[...] = thr
    for h in range(PEER_HEADS):
        g1_ref[h] = g1_ref[h] * rz[h:h + 1, :]


def _peer_topk(st, tb):
    nk, _, n = st.shape
    heads = nk // 2
    g_spec = pl.BlockSpec((heads, N_KEYS, tb), lambda i: (0, 0, i))
    g_shape = jax.ShapeDtypeStruct((heads, N_KEYS, n), F32)
    return pl.pallas_call(
        _topk_kernel,
        grid=(n // tb,),
        in_specs=[pl.BlockSpec((nk, N_KEYS, tb), lambda i: (0, 0, i))],
        out_specs=[g_spec, g_spec, pl.BlockSpec((heads, tb), lambda i: (0, i))],
        out_shape=[g_shape, g_shape, jax.ShapeDtypeStruct((heads, n), F32)],
        scratch_shapes=[pltpu.VMEM((PEER_TOPK, heads, tb), F32), pltpu.VMEM((PEER_TOPK, heads, tb), F32)],
        compiler_params=pltpu.CompilerParams(dimension_semantics=("parallel",), vmem_limit_bytes=VMEM_LIMIT),
        name="peer_topk",
    )(st)


def _transpose_cast_kernel(x_ref, o_ref):
    o_ref[...] = x_ref[...].T.astype(o_ref.dtype)


def _transpose_cast(x, dtype, blk):
    r, c = x.shape
    return pl.pallas_call(
        _transpose_cast_kernel,
        grid=(r // blk, c // blk),
        in_specs=[pl.BlockSpec((blk, blk), lambda i, j: (i, j))],
        out_specs=pl.BlockSpec((blk, blk), lambda i, j: (j, i)),
        out_shape=jax.ShapeDtypeStruct((c, r), dtype),
        compiler_params=pltpu.CompilerParams(dimension_semantics=("parallel", "parallel")),
        name="transpose_cast",
    )(x)


_GELU_C0 = 0.7978845608028654
_GELU_C1 = _GELU_C0 * 0.044715


def _peer_kernel(u_ref, vt_ref, ht_ref, g1_ref, g2_ref, thr_ref, xmid_ref, gfin_ref, y_ref,
                 acc, act0, act1, gate0, gate1):
    e = pl.program_id(1)
    n_i1 = u_ref.shape[0] // N_KEYS
    tt = act0.shape[1]
    nt = ht_ref.shape[1] // tt

    @pl.when(e == 0)
    def _():
        acc[...] = jnp.zeros_like(acc)

    def tile(t):
        return slice(t * tt, (t + 1) * tt)

    def act_matmul(buf, t):
        buf[...] = jnp.dot(u_ref[...], ht_ref[:, tile(t)], preferred_element_type=F32)

    def out_matmul(buf, t):
        acc[:, tile(t)] += jnp.dot(vt_ref[...], buf[...], preferred_element_type=F32)

    def gating(act_buf, gate_buf, t):
        for sub in range(tt // LANES):
            tok = slice(t * tt + sub * LANES, t * tt + (sub + 1) * LANES)
            cols = slice(sub * LANES, (sub + 1) * LANES)
            for il in range(n_i1):
                w = jnp.zeros((N_KEYS, LANES), F32)
                for h in range(PEER_HEADS):
                    prod = g1_ref[h, il:il + 1, tok] * g2_ref[h, :, tok]
                    w = w + jnp.where(prod >= thr_ref[h:h + 1, tok], prod, 0.0)
                rows = slice(il * N_KEYS, (il + 1) * N_KEYS)
                x = act_buf[rows, cols]
                t_ = jnp.tanh(x * (_GELU_C0 + _GELU_C1 * (x * x)))
                gate_buf[rows, cols] = ((w * x) * (1.0 + t_)).astype(BF16)

    acts, gates = (act0, act1), (gate0, gate1)
    act_matmul(acts[0], 0)
    for t in range(nt):
        if t + 1 < nt:
            act_matmul(acts[(t + 1) % 2], t + 1)
        gating(acts[t % 2], gates[t % 2], t)
        if t >= 1:
            out_matmul(gates[(t - 1) % 2], t - 1)
    out_matmul(gates[(nt - 1) % 2], nt - 1)

    @pl.when(e == pl.num_programs(1) - 1)
    def _():
        y_ref[...] = _rms(xmid_ref[...] + acc[...].T, gfin_ref[...])


def _peer(u_bf16, vt_bf16, ht, g1, g2, thr, xmid, gfin, tb, eb):
    n = xmid.shape[0]
    heads = g1.shape[0]
    tt = min(tb, 2 * LANES)
    return pl.pallas_call(
        _peer_kernel,
        grid=(n // tb, N_EXPERTS // eb),
        in_specs=[
            pl.BlockSpec((eb, D_MODEL), lambda t, e: (e, 0)),
            pl.BlockSpec((D_MODEL, eb), lambda t, e: (0, e)),
            pl.BlockSpec((D_MODEL, tb), lambda t, e: (0, t)),
            pl.BlockSpec((heads, eb // N_KEYS, tb), lambda t, e: (0, e, t)),
            pl.BlockSpec((heads, N_KEYS, tb), lambda t, e: (0, 0, t)),
            pl.BlockSpec((heads, tb), lambda t, e: (0, t)),
            pl.BlockSpec((tb, D_MODEL), lambda t, e: (t, 0)),
            pl.BlockSpec((1, D_MODEL), lambda t, e: (0, 0)),
        ],
        out_specs=pl.BlockSpec((tb, D_MODEL), lambda t, e: (t, 0)),
        out_shape=jax.ShapeDtypeStruct((n, D_MODEL), F32),
        scratch_shapes=[pltpu.VMEM((D_MODEL, tb), F32), pltpu.VMEM((eb, tt), F32), pltpu.VMEM((eb, tt), F32),
                        pltpu.VMEM((eb, tt), BF16), pltpu.VMEM((eb, tt), BF16)],
        compiler_params=pltpu.CompilerParams(
            dimension_semantics=("parallel", "arbitrary"), vmem_limit_bytes=VMEM_LIMIT),
        name="peer_dense",
    )(u_bf16, vt_bf16, ht, g1, g2, thr, xmid, gfin)


def _ffn_and_norm(oa, ob, x, params, tm, tb_topk, tb, eb):
    xmid, ht, st = _mix(oa, ob, x, params["ga"], params["gb"], params["wo"], params["gf"], params["wq"],
                        params["keys"], tm)
    g1, g2, thr = _peer_topk(st, tb_topk)
    return _peer(params["u"], params["vt"], ht, g1, g2, thr, xmid, params["gfin"], tb, eb)


def kernel(x_prompt, x_sample, cache_swa_kv, cache_dil_kv, attn_norm, w_in, sinks, out_norm_a, out_norm_b, w_out,
           ffn_norm, peer_w_q, peer_sub_keys, peer_u, peer_v, final_norm):
    depth = w_in.shape[0]
    assert depth == 1
    batch, seq, _ = x_prompt.shape
    n_s, s_len, _ = x_sample.shape
    assert s_len == 1 and seq % CHUNK == 0
    win_a = min(BLOCK, seq)
    win_b = min(CHUNK, seq)

    w_in_b = w_in[0].astype(BF16)
    g_attn = attn_norm[0][None]
    params = dict(
        ga=out_norm_a[0][None], gb=out_norm_b[0][None], wo=w_out[0].astype(BF16), gf=ffn_norm[0][None],
        wq=peer_w_q[0].astype(BF16),
        keys=peer_sub_keys[0].reshape(2 * PEER_HEADS, N_KEYS, PEER_QDIM // 2).astype(BF16),
        u=peer_u[0].astype(BF16), vt=_transpose_cast(peer_v[0], BF16, 1024), gfin=final_norm[None])
    sink_row = jnp.repeat(sinks[0].astype(F32), HEAD_DIM)[None]
    to_rows = lambda a: jnp.transpose(a, (0, 4, 1, 2, 3))[None]
    to_lanes = lambda a: jnp.transpose(a[0], (0, 2, 3, 4, 1))

    xp = x_prompt.reshape(batch * seq, D_MODEL)
    qa, kva, qb, kvb = _project(xp, jnp.arange(seq), g_attn, w_in_b, 512)
    oa, ob = _attn_prompt(qa, kva, qb, kvb, sink_row, batch, seq)
    y_prompt = _ffn_and_norm(oa, ob, xp, params, 256, 128, 1024, 1024).reshape(batch, seq, D_MODEL)
    swa_p = to_rows(_transpose_tail(kva.reshape(batch, seq, 2 * K_A), win_a).reshape(batch, 2, KV_A, HEAD_DIM, win_a))
    dil_p = to_rows(_transpose_tail(kvb.reshape(batch, seq, 2 * Q_B), win_b).reshape(batch, 2, H_B, HEAD_DIM, win_b))

    xs = x_sample.reshape(n_s, D_MODEL)
    pos_s = jnp.full((n_s,), PAST_LEN, jnp.int32)
    qa_s, kva_s, qb_s, kvb_s = _project(xs, pos_s, g_attn, w_in_b, n_s)
    swa_new, dil_new, oa_t, ob_t = _sample_attn(sinks[0].astype(F32), to_lanes(cache_swa_kv), to_lanes(cache_dil_kv),
                                                qa_s.T, kva_s.T, qb_s.T, kvb_s.T)
    y_sample = _ffn_and_norm(oa_t.T, ob_t.T, xs, params, n_s, 128, 128, 1024).reshape(n_s, 1, D_MODEL)
    return y_prompt, y_sample, swa_p, dil_p, to_rows(swa_new), to_rows(dil_new)
```

```python
import functools

import jax
import jax.numpy as jnp
from jax import lax
from jax.experimental import pallas as pl
from jax.experimental.pallas import tpu as pltpu

F32 = jnp.float32
BF16 = jnp.bfloat16

D_MODEL = 1024
HEAD_DIM = 64
PAST_LEN = 8192
H_A = 8
KV_A = 2
H_B = 8
Q_A = H_A * HEAD_DIM
K_A = KV_A * HEAD_DIM
Q_B = H_B * HEAD_DIM
D_IN = Q_A + 2 * K_A + 3 * Q_B
ROT_DIM = HEAD_DIM // 4
ROPE_THETA = 500000.0
BLOCK = 128
DILATIONS = (1, 4, 16)
CHUNK = BLOCK * DILATIONS[-1]
EPS = 1e-6
LANES = 128
SUBLANES = 8
NEG = -1e30

N_KEYS = 128
PEER_HEADS = 8
PEER_TOPK = 16
PEER_QDIM = 256
N_EXPERTS = N_KEYS * N_KEYS

VMEM_LIMIT = 56 * 1024 * 1024


def _rms(x, g):
    return (x * lax.rsqrt(jnp.mean(x * x, axis=-1, keepdims=True) + EPS)) * g


def _rope_tables(pos):
    inv = ROPE_THETA ** (-jnp.arange(0, ROT_DIM, 2, dtype=F32) / ROT_DIM)
    ang = pos.astype(F32)[:, None] * inv[None, :]
    cos, sin = jnp.cos(ang), jnp.sin(ang)
    t = pos.shape[0]
    half = ROT_DIM // 2
    rest = HEAD_DIM - ROT_DIM
    c = jnp.concatenate([cos, cos, jnp.ones((t, rest), F32)], axis=-1)
    s1 = jnp.concatenate([-sin, jnp.zeros((t, half + rest), F32)], axis=-1)
    s2 = jnp.concatenate([jnp.zeros((t, half), F32), sin, jnp.zeros((t, rest), F32)], axis=-1)
    rep = LANES // HEAD_DIM
    return jnp.tile(c, (1, rep)), jnp.tile(s1, (1, rep)), jnp.tile(s2, (1, rep))


def _proj_kernel(x_ref, g_ref, w_ref, c_ref, s1_ref, s2_ref, qa_ref, kva_ref, qb_ref, kvb_ref):
    h = _rms(x_ref[...], g_ref[...])
    z = jnp.dot(h.astype(BF16), w_ref[...], preferred_element_type=F32)
    c, s1, s2 = c_ref[...], s1_ref[...], s2_ref[...]
    half = ROT_DIM // 2

    def chunk(i, rot):
        zc = z[:, i * LANES:(i + 1) * LANES]
        if not rot:
            return zc
        return zc * c + pltpu.roll(zc, LANES - half, 1) * s1 + pltpu.roll(zc, half, 1) * s2

    nq = Q_A // LANES
    for i in range(nq):
        qa_ref[:, i * LANES:(i + 1) * LANES] = chunk(i, True)
    kva_ref[:, 0:LANES] = chunk(nq, True)
    kva_ref[:, LANES:2 * LANES] = chunk(nq + 1, False)
    base = nq + 2
    for i in range(nq):
        qb_ref[:, i * LANES:(i + 1) * LANES] = chunk(base + i, True)
    for i in range(nq):
        kvb_ref[:, i * LANES:(i + 1) * LANES] = chunk(base + nq + i, True)
    for i in range(nq):
        kvb_ref[:, Q_B + i * LANES:Q_B + (i + 1) * LANES] = chunk(base + 2 * nq + i, False)


def _project(x, pos, g, w_bf16, tm):
    n = x.shape[0]
    c, s1, s2 = _rope_tables(pos)
    per_seq = pos.shape[0] // tm
    row = lambda width: pl.BlockSpec((tm, width), lambda i: (i, 0))
    tab = pl.BlockSpec((tm, LANES), lambda i: (i % per_seq, 0))
    full = lambda a: pl.BlockSpec(a.shape, lambda i: (0, 0))
    return pl.pallas_call(
        _proj_kernel,
        grid=(n // tm,),
        in_specs=[row(D_MODEL), full(g), full(w_bf16), tab, tab, tab],
        out_specs=[row(Q_A), row(2 * K_A), row(Q_B), row(2 * Q_B)],
        out_shape=[jax.ShapeDtypeStruct((n, Q_A), F32), jax.ShapeDtypeStruct((n, 2 * K_A), F32),
                   jax.ShapeDtypeStruct((n, Q_B), F32), jax.ShapeDtypeStruct((n, 2 * Q_B), F32)],
        compiler_params=pltpu.CompilerParams(dimension_semantics=("parallel",), vmem_limit_bytes=VMEM_LIMIT),
        name="in_proj",
    )(x, g, w_bf16, c, s1, s2)


def _band_bias(with_prev):
    r = lax.broadcasted_iota(jnp.int32, (BLOCK, 2 * BLOCK), 0)
    k = lax.broadcasted_iota(jnp.int32, (BLOCK, 2 * BLOCK), 1)
    dist = r + BLOCK - k
    ok = (dist >= 0) & (dist <= BLOCK)
    if not with_prev:
        ok = ok & (k >= BLOCK)
    return jnp.where(ok, 0.0, NEG)


def _half_softmax_pv(qm, k, vm, bias):
    s = lax.dot_general(qm.astype(BF16), k.astype(BF16), (((1,), (1,)), ((), ())), preferred_element_type=F32) + bias
    m = jnp.max(s, axis=-1, keepdims=True)
    p = jnp.exp(s - m)
    l = jnp.sum(p, axis=-1, keepdims=True)
    o = jnp.dot(p.astype(BF16), vm.astype(BF16), preferred_element_type=F32)
    return o, m, l


def _attn_prompt_kernel(qa_ref, kva_c_ref, kva_p_ref, qb_ref, kb_c_ref, kb_p_ref, vb_c_ref, vb_p_ref, sink_ref,
                        oa_ref, ob_ref, kva_buf, kb_buf, vb_buf, o_buf, lse_buf, bias_buf):
    hp = pl.program_id(1)
    ck = pl.program_id(2)
    bias_buf[0] = _band_bias(False)
    bias_buf[1] = _band_bias(True)
    kva_buf[0:CHUNK] = kva_p_ref[0]
    kva_buf[CHUNK:2 * CHUNK] = kva_c_ref[0]
    kb_buf[0:CHUNK] = kb_p_ref[0]
    kb_buf[CHUNK:2 * CHUNK] = kb_c_ref[0]
    vb_buf[0:CHUNK] = vb_p_ref[0]
    vb_buf[CHUNK:2 * CHUNK] = vb_c_ref[0]

    lane = lax.broadcasted_iota(jnp.int32, (1, LANES), 1)
    in_half = [lane < HEAD_DIM, lane >= HEAD_DIM]
    kvh = hp // (H_A // KV_A // 2)
    kv_half = (lane >= kvh * HEAD_DIM) & (lane < (kvh + 1) * HEAD_DIM)
    sink = sink_ref[...]
    nblk = CHUNK // BLOCK
    scale = HEAD_DIM ** -0.5

    def body(i, carry):
        start = pl.multiple_of(i * BLOCK, BLOCK)
        bias = bias_buf[((ck > 0) | (i > 0)).astype(jnp.int32)]
        q = qa_ref[0, pl.ds(start, BLOCK), :] * scale
        kv = kva_buf[pl.ds(CHUNK - BLOCK + start, 2 * BLOCK), :]
        k = kv[:, 0:LANES]
        v = jnp.where(kv_half, kv[:, LANES:2 * LANES], 0.0)
        q_sw = pltpu.roll(q, HEAD_DIM, 1)
        acc = jnp.zeros((BLOCK, LANES), F32)
        for a in range(2):
            q_al = jnp.where(kvh == a, q, q_sw)
            qm = jnp.where(kv_half, q_al, 0.0)
            o, m, l = _half_softmax_pv(qm, k, v, bias)
            o = jnp.where(kvh == a, o, pltpu.roll(o, HEAD_DIM, 1))
            acc = acc + jnp.where(in_half[a], o / (l + jnp.exp(sink - m)), 0.0)
        oa_ref[0, pl.ds(start, BLOCK), :] = acc

        for bi, d in enumerate(DILATIONS):
            j, r = i // d, i % d
            qstart = j * (BLOCK * d) + r
            bias_b = bias_buf[((ck > 0) | (j > 0)).astype(jnp.int32)]
            if d == 1:
                qd = qb_ref[0, pl.ds(pl.multiple_of(qstart, BLOCK), BLOCK), :]
                kd = kb_buf[pl.ds(pl.multiple_of(CHUNK - BLOCK + qstart, BLOCK), 2 * BLOCK), :]
                vd = vb_buf[pl.ds(pl.multiple_of(CHUNK - BLOCK + qstart, BLOCK), 2 * BLOCK), :]
            else:
                qd = qb_ref[0, pl.ds(qstart, BLOCK, stride=d), :]
                kd = kb_buf[pl.ds(CHUNK - BLOCK * d + qstart, 2 * BLOCK, stride=d), :]
                vd = vb_buf[pl.ds(CHUNK - BLOCK * d + qstart, 2 * BLOCK, stride=d), :]
            o_acc = jnp.zeros((BLOCK, LANES), F32)
            lse_acc = jnp.zeros((BLOCK, LANES), F32)
            for a in range(2):
                qm = jnp.where(in_half[a], qd * scale, 0.0)
                vm = jnp.where(in_half[a], vd, 0.0)
                o, m, l = _half_softmax_pv(qm, kd, vm, bias_b)
                o_acc = o_acc + o / l
                lse_acc = lse_acc + jnp.where(in_half[a], m + jnp.log(l), 0.0)
            if d == 1:
                o_buf[bi, pl.ds(pl.multiple_of(qstart, BLOCK), BLOCK), :] = o_acc
                lse_buf[bi, pl.ds(pl.multiple_of(qstart, BLOCK), BLOCK), :] = lse_acc
            else:
                o_buf[bi, pl.ds(qstart, BLOCK, stride=d), :] = o_acc
                lse_buf[bi, pl.ds(qstart, BLOCK, stride=d), :] = lse_acc
        return carry

    lax.fori_loop(0, nblk, body, 0, unroll=2)

    l0, l1, l2 = lse_buf[0], lse_buf[1], lse_buf[2]
    mx = jnp.maximum(jnp.maximum(l0, l1), l2)
    w0, w1, w2 = jnp.exp(l0 - mx), jnp.exp(l1 - mx), jnp.exp(l2 - mx)
    tot = w0 + w1 + w2
    ob_ref[0] = (w0 / tot) * o_buf[0] + (w1 / tot) * o_buf[1] + (w2 / tot) * o_buf[2]


def _attn_prompt(qa, kva, qb, kvb, sink_row, batch, seq):
    qa = qa.reshape(batch, seq, Q_A)
    kva = kva.reshape(batch, seq, 2 * K_A)
    qb = qb.reshape(batch, seq, Q_B)
    kvb = kvb.reshape(batch, seq, 2 * Q_B)
    npair = Q_A // LANES
    cur = lambda b, h, c: (b, c, h)
    prev = lambda b, h, c: (b, jnp.maximum(c - 1, 0), h)
    blk = lambda width, fn: pl.BlockSpec((1, CHUNK, width), fn)
    oa, ob = pl.pallas_call(
        _attn_prompt_kernel,
        grid=(batch, npair, seq // CHUNK),
        in_specs=[
            blk(LANES, cur),
            blk(2 * K_A, lambda b, h, c: (b, c, 0)),
            blk(2 * K_A, lambda b, h, c: (b, jnp.maximum(c - 1, 0), 0)),
            blk(LANES, cur),
            blk(LANES, cur),
            blk(LANES, prev),
            blk(LANES, lambda b, h, c: (b, c, npair + h)),
            blk(LANES, lambda b, h, c: (b, jnp.maximum(c - 1, 0), npair + h)),
            pl.BlockSpec((1, LANES), lambda b, h, c: (0, h)),
        ],
        out_specs=[blk(LANES, cur), blk(LANES, cur)],
        out_shape=[jax.ShapeDtypeStruct((batch, seq, Q_A), F32), jax.ShapeDtypeStruct((batch, seq, Q_B), F32)],
        scratch_shapes=[
            pltpu.VMEM((2 * CHUNK, 2 * K_A), F32),
            pltpu.VMEM((2 * CHUNK, LANES), F32),
            pltpu.VMEM((2 * CHUNK, LANES), F32),
            pltpu.VMEM((len(DILATIONS), CHUNK, LANES), F32),
            pltpu.VMEM((len(DILATIONS), CHUNK, LANES), F32),
            pltpu.VMEM((2, BLOCK, 2 * BLOCK), F32),
        ],
        compiler_params=pltpu.CompilerParams(
            dimension_semantics=("parallel", "parallel", "arbitrary"), vmem_limit_bytes=VMEM_LIMIT),
        name="attn_prompt",
    )(qa, kva, kva, qb, kvb, kvb, kvb, kvb, sink_row)
    return oa.reshape(batch * seq, Q_A), ob.reshape(batch * seq, Q_B)


def _transpose_tail_kernel(x_ref, o_ref):
    o_ref[0] = x_ref[0].T


def _transpose_tail(x, rows):
    b, t, w = x.shape
    blk = min(rows, 512)
    first = (t - rows) // blk
    return pl.pallas_call(
        _transpose_tail_kernel,
        grid=(b, rows // blk),
        in_specs=[pl.BlockSpec((1, blk, w), lambda i, j: (i, first + j, 0))],
        out_specs=pl.BlockSpec((1, w, blk), lambda i, j: (i, 0, j)),
        out_shape=jax.ShapeDtypeStruct((b, w, rows), x.dtype),
        compiler_params=pltpu.CompilerParams(dimension_semantics=("parallel", "parallel")),
        name="transpose_tail",
    )(x)


def _sample_attn_kernel(sink_ref, swa_ref, dil_ref, qa_ref, kva_ref, qb_ref, kvb_ref,
                        swa_out_ref, dil_out_ref, oa_ref, ob_ref, sa_scr, selfa_scr, sb_scr, selfb_scr):
    n = pl.program_id(0)
    win = swa_ref.shape[-1]
    span = dil_ref.shape[-1]
    scale = HEAD_DIM ** -0.5

    @pl.when(n == 0)
    def _():
        oa_ref[...] = jnp.zeros_like(oa_ref)
        ob_ref[...] = jnp.zeros_like(ob_ref)

    mine = lax.broadcasted_iota(jnp.int32, (1, qa_ref.shape[1]), 1) == n

    picked = {}

    def pick(ref, start):
        if (id(ref), start) not in picked:
            picked[(id(ref), start)] = jnp.sum(jnp.where(mine, ref[start:start + HEAD_DIM, :], 0.0), axis=1,
                                               keepdims=True)
        return picked[(id(ref), start)]

    def put(ref, start, col):
        ref[start:start + HEAD_DIM, :] = jnp.where(mine, col, ref[start:start + HEAD_DIM, :])

    def attend(s_scr, self_scr, k_of, v_of, q_of, knew_of, vnew_of, weight, self_weight, extra, o_ref):
        for h in range(H_A):
            q = q_of(h)
            s_scr[h:h + 1, :] = jnp.sum(k_of(h) * q, axis=0, keepdims=True)
            self_scr[h:h + 1, :] = jnp.broadcast_to(jnp.sum(knew_of(h) * q, axis=0, keepdims=True), (1, LANES))
        s = s_scr[...] * scale
        s_self = self_scr[:, 0:1] * scale
        if weight is not None:
            s = jnp.where(weight > 0.0, s, NEG)
        m = jnp.maximum(jnp.max(s, axis=1, keepdims=True), s_self)
        p = jnp.exp(s - m)
        if weight is not None:
            p = p * weight
        p_self = self_weight * jnp.exp(s_self - m)
        l = jnp.sum(p, axis=1, keepdims=True) + p_self
        if extra is not None:
            l = l + jnp.exp(extra - m)
        s_scr[...] = p / l
        self_scr[...] = jnp.broadcast_to(p_self / l, self_scr.shape)
        for h in range(H_A):
            o = jnp.sum(v_of(h) * s_scr[h:h + 1, :], axis=1, keepdims=True) + vnew_of(h) * self_scr[h:h + 1, 0:1]
            put(o_ref, h * HEAD_DIM, o)

    def shifted(x, new):
        width = x.shape[-1]
        rolled = pltpu.roll(x, width - 1, 1)
        last = lax.broadcasted_iota(jnp.int32, (1, width), 1) == width - 1
        return jnp.where(last, new, rolled)

    group = H_A // KV_A
    head_id = lax.broadcasted_iota(jnp.int32, (H_A, 1), 0)
    sink = jnp.zeros((H_A, 1), F32)
    for hq in range(H_A):
        sink = jnp.where(head_id == hq, sink_ref[hq], sink)
    attend(sa_scr, selfa_scr,
           lambda h: swa_ref[0, 0, h // group], lambda h: swa_ref[0, 1, h // group],
           lambda h: pick(qa_ref, h * HEAD_DIM),
           lambda h: pick(kva_ref, (h // group) * HEAD_DIM), lambda h: pick(kva_ref, K_A + (h // group) * HEAD_DIM),
           None, 1.0, sink, oa_ref)
    for kv in range(2):
        for h in range(KV_A):
            swa_out_ref[0, kv, h] = shifted(swa_ref[0, kv, h], pick(kva_ref, kv * K_A + h * HEAD_DIM))

    pos = lax.broadcasted_iota(jnp.int32, (1, span), 1)
    weight = jnp.zeros((1, span), F32)
    for d in DILATIONS:
        weight = weight + jnp.where((pos % d == 0) & (pos >= span - BLOCK * d), 1.0, 0.0)
    attend(sb_scr, selfb_scr,
           lambda h: dil_ref[0, 0, h], lambda h: dil_ref[0, 1, h], lambda h: pick(qb_ref, h * HEAD_DIM),
           lambda h: pick(kvb_ref, h * HEAD_DIM), lambda h: pick(kvb_ref, Q_B + h * HEAD_DIM),
           weight, float(len(DILATIONS)), None, ob_ref)
    for h in range(H_B):
        dil_out_ref[0, 0, h] = shifted(dil_ref[0, 0, h], pick(kvb_ref, h * HEAD_DIM))
        dil_out_ref[0, 1, h] = shifted(dil_ref[0, 1, h], pick(kvb_ref, Q_B + h * HEAD_DIM))


def _sample_attn(sinks, swa, dil, qa_t, kva_t, qb_t, kvb_t):
    n = swa.shape[0]
    assert swa.shape[-1] == BLOCK and dil.shape[-1] == CHUNK
    per_sample = lambda a: pl.BlockSpec((1,) + a.shape[1:], lambda i: (i, 0, 0, 0, 0))
    full = lambda a: pl.BlockSpec(a.shape, lambda i: (0, 0))
    o_shape = jax.ShapeDtypeStruct((Q_A, n), F32)
    return pl.pallas_call(
        _sample_attn_kernel,
        grid=(n,),
        in_specs=[pl.BlockSpec(memory_space=pltpu.SMEM), per_sample(swa), per_sample(dil),
                  full(qa_t), full(kva_t), full(qb_t), full(kvb_t)],
        out_specs=[per_sample(swa), per_sample(dil), full(qa_t), full(qb_t)],
        out_shape=[jax.ShapeDtypeStruct(swa.shape, F32), jax.ShapeDtypeStruct(dil.shape, F32), o_shape, o_shape],
        scratch_shapes=[pltpu.VMEM((H_A, BLOCK), F32), pltpu.VMEM((H_A, LANES), F32),
                        pltpu.VMEM((H_B, CHUNK), F32), pltpu.VMEM((H_B, LANES), F32)],
        compiler_params=pltpu.CompilerParams(dimension_semantics=("arbitrary",), vmem_limit_bytes=VMEM_LIMIT),
        name="sample_attn",
    )(sinks, swa, dil, qa_t, kva_t, qb_t, kvb_t)


def _mix_kernel(oa_ref, ob_ref, x_ref, ga_ref, gb_ref, wo_ref, gf_ref, wq_ref, keys_ref, xmid_ref, ht_ref, st_ref):
    o = jnp.concatenate([_rms(oa_ref[...], ga_ref[...]), _rms(ob_ref[...], gb_ref[...])], axis=-1)
    xm = x_ref[...] + jnp.dot(o.astype(BF16), wo_ref[...], preferred_element_type=F32)
    xmid_ref[...] = xm
    h = _rms(xm, gf_ref[...])
    hb = h.astype(BF16)
    ht_ref[...] = h.T.astype(BF16)
    q = jnp.dot(hb, wq_ref[...], preferred_element_type=F32).astype(BF16)
    half = PEER_QDIM // 2
    for i in range(2 * PEER_HEADS):
        st_ref[i] = lax.dot_general(keys_ref[i], q[:, i * half:(i + 1) * half], (((1,), (1,)), ((), ())),
                                    preferred_element_type=F32)


def _mix(oa, ob, x, ga, gb, wo, gf, wq, keys, tm):
    n = x.shape[0]
    nk = keys.shape[0]
    row = lambda width: pl.BlockSpec((tm, width), lambda i: (i, 0))
    full = lambda a: pl.BlockSpec(a.shape, lambda i: (0,) * a.ndim)
    return pl.pallas_call(
        _mix_kernel,
        grid=(n // tm,),
        in_specs=[row(Q_A), row(Q_B), row(D_MODEL), full(ga), full(gb), full(wo), full(gf), full(wq), full(keys)],
        out_specs=[row(D_MODEL), pl.BlockSpec((D_MODEL, tm), lambda i: (0, i)),
                   pl.BlockSpec((nk, N_KEYS, tm), lambda i: (0, 0, i))],
        out_shape=[jax.ShapeDtypeStruct((n, D_MODEL), F32), jax.ShapeDtypeStruct((D_MODEL, n), BF16),
                   jax.ShapeDtypeStruct((nk, N_KEYS, n), F32)],
        compiler_params=pltpu.CompilerParams(dimension_semantics=("parallel",), vmem_limit_bytes=VMEM_LIMIT),
        name="mix_out",
    )(oa, ob, x, ga, gb, wo, gf, wq, keys)


_CAND = [(i, j) for i in range(PEER_TOPK) for j in range(PEER_TOPK) if (i + 1) * (j + 1) <= PEER_TOPK]


def _merge_exchange_network(n):
    pairs = []
    p = 1
    while p < n:
        k = p
        while k >= 1:
            for j in range(k % p, n - k, 2 * k):
                for i in range(min(k, n - j - k)):
                    if (i + j) // (2 * p) == (i + j + k) // (2 * p):
                        pairs.append((i + j, i + j + k))
            k //= 2
        p *= 2
    return pairs


_SORT_NET = _merge_exchange_network(N_KEYS // SUBLANES)


def _topk_kernel(s_ref, g1_ref, g2_ref, thr_ref, a_scr, b_scr):
    def per_head(h, carry):
        for p, (g_ref, scr) in enumerate(((g1_ref, a_scr), (g2_ref, b_scr))):
            s = s_ref[2 * h + p]
            e = jnp.exp(s - jnp.max(s, axis=0, keepdims=True))
            g_ref[h] = e
            rows = [e[i * SUBLANES:(i + 1) * SUBLANES, :] for i in range(N_KEYS // SUBLANES)]
            for i, j in _SORT_NET:
                rows[i], rows[j] = jnp.maximum(rows[i], rows[j]), jnp.minimum(rows[i], rows[j])
            for k in range(PEER_TOPK):
                mk = jnp.max(rows[0], axis=0, keepdims=True)
                scr[k, pl.ds(h, 1), :] = mk
                popped = rows[0] == mk
                for d in range(PEER_TOPK - 1 - k):
                    rows[d] = jnp.where(popped, rows[d + 1], rows[d])
        return carry

    lax.fori_loop(0, PEER_HEADS, per_head, 0)

    a = [a_scr[k] for k in range(PEER_TOPK)]
    b = [b_scr[k] for k in range(PEER_TOPK)]
    cand = {(i, j): a[i] * b[j] for i, j in _CAND}
    lists = [[cand[(i, j)] for j in range(PEER_TOPK // (i + 1))] for i in range(PEER_TOPK)]
    z = jnp.zeros_like(a[0])
    theta = z
    for k in range(PEER_TOPK):
        mk = functools.reduce(jnp.maximum, [lst[0] for lst in lists])
        z = z + jnp.maximum(mk, 0.0)
        theta = jnp.where(mk > 0.0, mk, theta)
        for lst in lists:
            popped = lst[0] == mk
            keep = min(len(lst), PEER_TOPK - k)
            for d in range(keep):
                lst[d] = jnp.where(popped, lst[d + 1] if d + 1 < len(lst) else -1.0, lst[d])
    rz = 0.5 / z
    thr = jnp.full_like(z, 2.0)
    for (i, j), c in cand.items():
        thr = jnp.minimum(thr, jnp.where(c >= theta, (a[i] * rz) * b[j], 2.0))
    thr_ref[...] = thr
    for h in range(PEER_HEADS):
        g1_ref[h] = g1_ref[h] * rz[h:h + 1, :]


def _peer_topk(st, tb):
    nk, _, n = st.shape
    heads = nk // 2
    g_spec = pl.BlockSpec((heads, N_KEYS, tb), lambda i: (0, 0, i))
    g_shape = jax.ShapeDtypeStruct((heads, N_KEYS, n), F32)
    return pl.pallas_call(
        _topk_kernel,
        grid=(n // tb,),
        in_specs=[pl.BlockSpec((nk, N_KEYS, tb), lambda i: (0, 0, i))],
        out_specs=[g_spec, g_spec, pl.BlockSpec((heads, tb), lambda i: (0, i))],
        out_shape=[g_shape, g_shape, jax.ShapeDtypeStruct((heads, n), F32)],
        scratch_shapes=[pltpu.VMEM((PEER_TOPK, heads, tb), F32), pltpu.VMEM((PEER_TOPK, heads, tb), F32)],
        compiler_params=pltpu.CompilerParams(dimension_semantics=("parallel",), vmem_limit_bytes=VMEM_LIMIT),
        name="peer_topk",
    )(st)


def _transpose_cast_kernel(x_ref, o_ref):
    o_ref[...] = x_ref[...].T.astype(o_ref.dtype)


def _transpose_cast(x, dtype, blk):
    r, c = x.shape
    return pl.pallas_call(
        _transpose_cast_kernel,
        grid=(r // blk, c // blk),
        in_specs=[pl.BlockSpec((blk, blk), lambda i, j: (i, j))],
        out_specs=pl.BlockSpec((blk, blk), lambda i, j: (j, i)),
        out_shape=jax.ShapeDtypeStruct((c, r), dtype),
        compiler_params=pltpu.CompilerParams(dimension_semantics=("parallel", "parallel")),
        name="transpose_cast",
    )(x)


_GELU_C0 = 0.7978845608028654
_GELU_C1 = _GELU_C0 * 0.044715


def _peer_kernel(u_ref, vt_ref, ht_ref, g1_ref, g2_ref, thr_ref, xmid_ref, gfin_ref, y_ref,
                 acc, act0, act1, gate0, gate1):
    e = pl.program_id(1)
    n_i1 = u_ref.shape[0] // N_KEYS
    tt = act0.shape[1]
    nt = ht_ref.shape[1] // tt

    @pl.when(e == 0)
    def _():
        acc[...] = jnp.zeros_like(acc)

    def tile(t):
        return slice(t * tt, (t + 1) * tt)

    def act_matmul(buf, t):
        buf[...] = jnp.dot(u_ref[...], ht_ref[:, tile(t)], preferred_element_type=F32)

    def out_matmul(buf, t):
        acc[:, tile(t)] += jnp.dot(vt_ref[...], buf[...], preferred_element_type=F32)

    def gating(act_buf, gate_buf, t):
        for sub in range(tt // LANES):
            tok = slice(t * tt + sub * LANES, t * tt + (sub + 1) * LANES)
            cols = slice(sub * LANES, (sub + 1) * LANES)
            for il in range(n_i1):
                w = jnp.zeros((N_KEYS, LANES), F32)
                for h in range(PEER_HEADS):
                    prod = g1_ref[h, il:il + 1, tok] * g2_ref[h, :, tok]
                    w = w + jnp.where(prod >= thr_ref[h:h + 1, tok], prod, 0.0)
                rows = slice(il * N_KEYS, (il + 1) * N_KEYS)
                x = act_buf[rows, cols].astype(BF16)
                t_ = jnp.tanh(x * (_GELU_C0 + _GELU_C1 * (x * x)))
                gate_buf[rows, cols] = (w.astype(BF16) * x) * (1.0 + t_)

    acts, gates = (act0, act1), (gate0, gate1)
    act_matmul(acts[0], 0)
    for t in range(nt):
        if t + 1 < nt:
            act_matmul(acts[(t + 1) % 2], t + 1)
        gating(acts[t % 2], gates[t % 2], t)
        if t >= 1:
            out_matmul(gates[(t - 1) % 2], t - 1)
    out_matmul(gates[(nt - 1) % 2], nt - 1)

    @pl.when(e == pl.num_programs(1) - 1)
    def _():
        y_ref[...] = _rms(xmid_ref[...] + acc[...].T, gfin_ref[...])


def _peer(u_bf16, vt_bf16, ht, g1, g2, thr, xmid, gfin, tb, eb):
    n = xmid.shape[0]
    heads = g1.shape[0]
    tt = min(tb, 2 * LANES)
    return pl.pallas_call(
        _peer_kernel,
        grid=(n // tb, N_EXPERTS // eb),
        in_specs=[
            pl.BlockSpec((eb, D_MODEL), lambda t, e: (e, 0)),
            pl.BlockSpec((D_MODEL, eb), lambda t, e: (0, e)),
            pl.BlockSpec((D_MODEL, tb), lambda t, e: (0, t)),
            pl.BlockSpec((heads, eb // N_KEYS, tb), lambda t, e: (0, e, t)),
            pl.BlockSpec((heads, N_KEYS, tb), lambda t, e: (0, 0, t)),
            pl.BlockSpec((heads, tb), lambda t, e: (0, t)),
            pl.BlockSpec((tb, D_MODEL), lambda t, e: (t, 0)),
            pl.BlockSpec((1, D_MODEL), lambda t, e: (0, 0)),
        ],
        out_specs=pl.BlockSpec((tb, D_MODEL), lambda t, e: (t, 0)),
        out_shape=jax.ShapeDtypeStruct((n, D_MODEL), F32),
        scratch_shapes=[pltpu.VMEM((D_MODEL, tb), F32), pltpu.VMEM((eb, tt), F32), pltpu.VMEM((eb, tt), F32),
                        pltpu.VMEM((eb, tt), BF16), pltpu.VMEM((eb, tt), BF16)],
        compiler_params=pltpu.CompilerParams(
            dimension_semantics=("parallel", "arbitrary"), vmem_limit_bytes=VMEM_LIMIT),
        name="peer_dense",
    )(u_bf16, vt_bf16, ht, g1, g2, thr, xmid, gfin)


def _ffn_and_norm(oa, ob, x, params, tm, tb_topk, tb, eb):
    xmid, ht, st = _mix(oa, ob, x, params["ga"], params["gb"], params["wo"], params["gf"], params["wq"],
                        params["keys"], tm)
    g1, g2, thr = _peer_topk(st, tb_topk)
    return _peer(params["u"], params["vt"], ht, g1, g2, thr, xmid, params["gfin"], tb, eb)


def kernel(x_prompt, x_sample, cache_swa_kv, cache_dil_kv, attn_norm, w_in, sinks, out_norm_a, out_norm_b, w_out,
           ffn_norm, peer_w_q, peer_sub_keys, peer_u, peer_v, final_norm):
    depth = w_in.shape[0]
    assert depth == 1
    batch, seq, _ = x_prompt.shape
    n_s, s_len, _ = x_sample.shape
    assert s_len == 1 and seq % CHUNK == 0
    win_a = min(BLOCK, seq)
    win_b = min(CHUNK, seq)

    w_in_b = w_in[0].astype(BF16)
    g_attn = attn_norm[0][None]
    params = dict(
        ga=out_norm_a[0][None], gb=out_norm_b[0][None], wo=w_out[0].astype(BF16), gf=ffn_norm[0][None],
        wq=peer_w_q[0].astype(BF16),
        keys=peer_sub_keys[0].reshape(2 * PEER_HEADS, N_KEYS, PEER_QDIM // 2).astype(BF16),
        u=peer_u[0].astype(BF16), vt=_transpose_cast(peer_v[0], BF16, 1024), gfin=final_norm[None])
    sink_row = jnp.repeat(sinks[0].astype(F32), HEAD_DIM)[None]
    to_rows = lambda a: jnp.transpose(a, (0, 4, 1, 2, 3))[None]
    to_lanes = lambda a: jnp.transpose(a[0], (0, 2, 3, 4, 1))

    xp = x_prompt.reshape(batch * seq, D_MODEL)
    qa, kva, qb, kvb = _project(xp, jnp.arange(seq), g_attn, w_in_b, 512)
    oa, ob = _attn_prompt(qa, kva, qb, kvb, sink_row, batch, seq)
    y_prompt = _ffn_and_norm(oa, ob, xp, params, 256, 256, 1024, 1024).reshape(batch, seq, D_MODEL)
    swa_p = to_rows(_transpose_tail(kva.reshape(batch, seq, 2 * K_A), win_a).reshape(batch, 2, KV_A, HEAD_DIM, win_a))
    dil_p = to_rows(_transpose_tail(kvb.reshape(batch, seq, 2 * Q_B), win_b).reshape(batch, 2, H_B, HEAD_DIM, win_b))

    xs = x_sample.reshape(n_s, D_MODEL)
    pos_s = jnp.full((n_s,), PAST_LEN, jnp.int32)
    qa_s, kva_s, qb_s, kvb_s = _project(xs, pos_s, g_attn, w_in_b, n_s)
    swa_new, dil_new, oa_t, ob_t = _sample_attn(sinks[0].astype(F32), to_lanes(cache_swa_kv), to_lanes(cache_dil_kv),
                                                qa_s.T, kva_s.T, qb_s.T, kvb_s.T)
    y_sample = _ffn_and_norm(oa_t.T, ob_t.T, xs, params, n_s, 128, 128, 1024).reshape(n_s, 1, D_MODEL)
    return y_prompt, y_sample, swa_p, dil_p, to_rows(swa_new), to_rows(dil_new)
```

```python
import functools

import jax
import jax.numpy as jnp
from jax import lax
from jax.experimental import pallas as pl
from jax.experimental.pallas import tpu as pltpu

F32 = jnp.float32
BF16 = jnp.bfloat16

D_MODEL = 1024
HEAD_DIM = 64
PAST_LEN = 8192
H_A = 8
KV_A = 2
H_B = 8
Q_A = H_A * HEAD_DIM
K_A = KV_A * HEAD_DIM
Q_B = H_B * HEAD_DIM
D_IN = Q_A + 2 * K_A + 3 * Q_B
ROT_DIM = HEAD_DIM // 4
ROPE_THETA = 500000.0
BLOCK = 128
DILATIONS = (1, 4, 16)
CHUNK = BLOCK * DILATIONS[-1]
EPS = 1e-6
LANES = 128
SUBLANES = 8
NEG = -1e30

N_KEYS = 128
PEER_HEADS = 8
PEER_TOPK = 16
PEER_QDIM = 256
N_EXPERTS = N_KEYS * N_KEYS

VMEM_LIMIT = 56 * 1024 * 1024


def _rms(x, g):
    return (x * lax.rsqrt(jnp.mean(x * x, axis=-1, keepdims=True) + EPS)) * g


def _rope_tables(pos):
    inv = ROPE_THETA ** (-jnp.arange(0, ROT_DIM, 2, dtype=F32) / ROT_DIM)
    ang = pos.astype(F32)[:, None] * inv[None, :]
    cos, sin = jnp.cos(ang), jnp.sin(ang)
    t = pos.shape[0]
    half = ROT_DIM // 2
    rest = HEAD_DIM - ROT_DIM
    c = jnp.concatenate([cos, cos, jnp.ones((t, rest), F32)], axis=-1)
    s1 = jnp.concatenate([-sin, jnp.zeros((t, half + rest), F32)], axis=-1)
    s2 = jnp.concatenate([jnp.zeros((t, half), F32), sin, jnp.zeros((t, rest), F32)], axis=-1)
    rep = LANES // HEAD_DIM
    return jnp.tile(c, (1, rep)), jnp.tile(s1, (1, rep)), jnp.tile(s2, (1, rep))


def _proj_kernel(x_ref, g_ref, w_ref, c_ref, s1_ref, s2_ref, qa_ref, kva_ref, qb_ref, kvb_ref):
    h = _rms(x_ref[...], g_ref[...])
    z = jnp.dot(h.astype(BF16), w_ref[...], preferred_element_type=F32)
    c, s1, s2 = c_ref[...], s1_ref[...], s2_ref[...]
    half = ROT_DIM // 2

    def chunk(i, rot):
        zc = z[:, i * LANES:(i + 1) * LANES]
        if not rot:
            return zc
        return zc * c + pltpu.roll(zc, LANES - half, 1) * s1 + pltpu.roll(zc, half, 1) * s2

    nq = Q_A // LANES
    for i in range(nq):
        qa_ref[:, i * LANES:(i + 1) * LANES] = chunk(i, True)
    kva_ref[:, 0:LANES] = chunk(nq, True)
    kva_ref[:, LANES:2 * LANES] = chunk(nq + 1, False)
    base = nq + 2
    for i in range(nq):
        qb_ref[:, i * LANES:(i + 1) * LANES] = chunk(base + i, True)
    for i in range(nq):
        kvb_ref[:, i * LANES:(i + 1) * LANES] = chunk(base + nq + i, True)
    for i in range(nq):
        kvb_ref[:, Q_B + i * LANES:Q_B + (i + 1) * LANES] = chunk(base + 2 * nq + i, False)


def _project(x, pos, g, w_bf16, tm):
    n = x.shape[0]
    c, s1, s2 = _rope_tables(pos)
    per_seq = pos.shape[0] // tm
    row = lambda width: pl.BlockSpec((tm, width), lambda i: (i, 0))
    tab = pl.BlockSpec((tm, LANES), lambda i: (i % per_seq, 0))
    full = lambda a: pl.BlockSpec(a.shape, lambda i: (0, 0))
    return pl.pallas_call(
        _proj_kernel,
        grid=(n // tm,),
        in_specs=[row(D_MODEL), full(g), full(w_bf16), tab, tab, tab],
        out_specs=[row(Q_A), row(2 * K_A), row(Q_B), row(2 * Q_B)],
        out_shape=[jax.ShapeDtypeStruct((n, Q_A), F32), jax.ShapeDtypeStruct((n, 2 * K_A), F32),
                   jax.ShapeDtypeStruct((n, Q_B), F32), jax.ShapeDtypeStruct((n, 2 * Q_B), F32)],
        compiler_params=pltpu.CompilerParams(dimension_semantics=("parallel",), vmem_limit_bytes=VMEM_LIMIT),
        name="in_proj",
    )(x, g, w_bf16, c, s1, s2)


def _band_bias(with_prev):
    r = lax.broadcasted_iota(jnp.int32, (BLOCK, 2 * BLOCK), 0)
    k = lax.broadcasted_iota(jnp.int32, (BLOCK, 2 * BLOCK), 1)
    dist = r + BLOCK - k
    ok = (dist >= 0) & (dist <= BLOCK)
    if not with_prev:
        ok = ok & (k >= BLOCK)
    return jnp.where(ok, 0.0, NEG)


def _half_softmax_pv(qm, k, vm, bias):
    s = lax.dot_general(qm.astype(BF16), k.astype(BF16), (((1,), (1,)), ((), ())), preferred_element_type=F32) + bias
    m = jnp.max(s, axis=-1, keepdims=True)
    p = jnp.exp(s - m)
    l = jnp.sum(p, axis=-1, keepdims=True)
    o = jnp.dot(p.astype(BF16), vm.astype(BF16), preferred_element_type=F32)
    return o, m, l


def _attn_prompt_kernel(qa_ref, kva_c_ref, kva_p_ref, qb_ref, kb_c_ref, kb_p_ref, vb_c_ref, vb_p_ref, sink_ref,
                        oa_ref, ob_ref, kva_buf, kb_buf, vb_buf, o_buf, lse_buf, bias_buf):
    hp = pl.program_id(1)
    ck = pl.program_id(2)
    bias_buf[0] = _band_bias(False)
    bias_buf[1] = _band_bias(True)
    kva_buf[0:CHUNK] = kva_p_ref[0]
    kva_buf[CHUNK:2 * CHUNK] = kva_c_ref[0]
    kb_buf[0:CHUNK] = kb_p_ref[0]
    kb_buf[CHUNK:2 * CHUNK] = kb_c_ref[0]
    vb_buf[0:CHUNK] = vb_p_ref[0]
    vb_buf[CHUNK:2 * CHUNK] = vb_c_ref[0]

    lane = lax.broadcasted_iota(jnp.int32, (1, LANES), 1)
    in_half = [lane < HEAD_DIM, lane >= HEAD_DIM]
    kvh = hp // (H_A // KV_A // 2)
    kv_half = (lane >= kvh * HEAD_DIM) & (lane < (kvh + 1) * HEAD_DIM)
    sink = sink_ref[...]
    nblk = CHUNK // BLOCK
    scale = HEAD_DIM ** -0.5

    def body(i, carry):
        start = pl.multiple_of(i * BLOCK, BLOCK)
        bias = bias_buf[((ck > 0) | (i > 0)).astype(jnp.int32)]
        q = qa_ref[0, pl.ds(start, BLOCK), :] * scale
        kv = kva_buf[pl.ds(CHUNK - BLOCK + start, 2 * BLOCK), :]
        k = kv[:, 0:LANES]
        v = jnp.where(kv_half, kv[:, LANES:2 * LANES], 0.0)
        q_sw = pltpu.roll(q, HEAD_DIM, 1)
        acc = jnp.zeros((BLOCK, LANES), F32)
        for a in range(2):
            q_al = jnp.where(kvh == a, q, q_sw)
            qm = jnp.where(kv_half, q_al, 0.0)
            o, m, l = _half_softmax_pv(qm, k, v, bias)
            o = jnp.where(kvh == a, o, pltpu.roll(o, HEAD_DIM, 1))
            acc = acc + jnp.where(in_half[a], o / (l + jnp.exp(sink - m)), 0.0)
        oa_ref[0, pl.ds(start, BLOCK), :] = acc

        for bi, d in enumerate(DILATIONS):
            j, r = i // d, i % d
            qstart = j * (BLOCK * d) + r
            bias_b = bias_buf[((ck > 0) | (j > 0)).astype(jnp.int32)]
            if d == 1:
                qd = qb_ref[0, pl.ds(pl.multiple_of(qstart, BLOCK), BLOCK), :]
                kd = kb_buf[pl.ds(pl.multiple_of(CHUNK - BLOCK + qstart, BLOCK), 2 * BLOCK), :]
                vd = vb_buf[pl.ds(pl.multiple_of(CHUNK - BLOCK + qstart, BLOCK), 2 * BLOCK), :]
            else:
                qd = qb_ref[0, pl.ds(qstart, BLOCK, stride=d), :]
                kd = kb_buf[pl.ds(CHUNK - BLOCK * d + qstart, 2 * BLOCK, stride=d), :]
                vd = vb_buf[pl.ds(CHUNK - BLOCK * d + qstart, 2 * BLOCK, stride=d), :]
            o_acc = jnp.zeros((BLOCK, LANES), F32)
            lse_acc = jnp.zeros((BLOCK, LANES), F32)
            for a in range(2):
                qm = jnp.where(in_half[a], qd * scale, 0.0)
                vm = jnp.where(in_half[a], vd, 0.0)
                o, m, l = _half_softmax_pv(qm, kd, vm, bias_b)
                o_acc = o_acc + o / l
                lse_acc = lse_acc + jnp.where(in_half[a], m + jnp.log(l), 0.0)
            if d == 1:
                o_buf[bi, pl.ds(pl.multiple_of(qstart, BLOCK), BLOCK), :] = o_acc
                lse_buf[bi, pl.ds(pl.multiple_of(qstart, BLOCK), BLOCK), :] = lse_acc
            else:
                o_buf[bi, pl.ds(qstart, BLOCK, stride=d), :] = o_acc
                lse_buf[bi, pl.ds(qstart, BLOCK, stride=d), :] = lse_acc
        return carry

    lax.fori_loop(0, nblk, body, 0, unroll=2)

    l0, l1, l2 = lse_buf[0], lse_buf[1], lse_buf[2]
    mx = jnp.maximum(jnp.maximum(l0, l1), l2)
    w0, w1, w2 = jnp.exp(l0 - mx), jnp.exp(l1 - mx), jnp.exp(l2 - mx)
    tot = w0 + w1 + w2
    ob_ref[0] = (w0 / tot) * o_buf[0] + (w1 / tot) * o_buf[1] + (w2 / tot) * o_buf[2]


def _attn_prompt(qa, kva, qb, kvb, sink_row, batch, seq):
    qa = qa.reshape(batch, seq, Q_A)
    kva = kva.reshape(batch, seq, 2 * K_A)
    qb = qb.reshape(batch, seq, Q_B)
    kvb = kvb.reshape(batch, seq, 2 * Q_B)
    npair = Q_A // LANES
    cur = lambda b, h, c: (b, c, h)
    prev = lambda b, h, c: (b, jnp.maximum(c - 1, 0), h)
    blk = lambda width, fn: pl.BlockSpec((1, CHUNK, width), fn)
    oa, ob = pl.pallas_call(
        _attn_prompt_kernel,
        grid=(batch, npair, seq // CHUNK),
        in_specs=[
            blk(LANES, cur),
            blk(2 * K_A, lambda b, h, c: (b, c, 0)),
            blk(2 * K_A, lambda b, h, c: (b, jnp.maximum(c - 1, 0), 0)),
            blk(LANES, cur),
            blk(LANES, cur),
            blk(LANES, prev),
            blk(LANES, lambda b, h, c: (b, c, npair + h)),
            blk(LANES, lambda b, h, c: (b, jnp.maximum(c - 1, 0), npair + h)),
            pl.BlockSpec((1, LANES), lambda b, h, c: (0, h)),
        ],
        out_specs=[blk(LANES, cur), blk(LANES, cur)],
        out_shape=[jax.ShapeDtypeStruct((batch, seq, Q_A), F32), jax.ShapeDtypeStruct((batch, seq, Q_B), F32)],
        scratch_shapes=[
            pltpu.VMEM((2 * CHUNK, 2 * K_A), F32),
            pltpu.VMEM((2 * CHUNK, LANES), F32),
            pltpu.VMEM((2 * CHUNK, LANES), F32),
            pltpu.VMEM((len(DILATIONS), CHUNK, LANES), F32),
            pltpu.VMEM((len(DILATIONS), CHUNK, LANES), F32),
            pltpu.VMEM((2, BLOCK, 2 * BLOCK), F32),
        ],
        compiler_params=pltpu.CompilerParams(
            dimension_semantics=("parallel", "parallel", "arbitrary"), vmem_limit_bytes=VMEM_LIMIT),
        name="attn_prompt",
    )(qa, kva, kva, qb, kvb, kvb, kvb, kvb, sink_row)
    return oa.reshape(batch * seq, Q_A), ob.reshape(batch * seq, Q_B)


def _transpose_tail_kernel(x_ref, o_ref):
    o_ref[0] = x_ref[0].T


def _transpose_tail(x, rows):
    b, t, w = x.shape
    blk = min(rows, 512)
    first = (t - rows) // blk
    return pl.pallas_call(
        _transpose_tail_kernel,
        grid=(b, rows // blk),
        in_specs=[pl.BlockSpec((1, blk, w), lambda i, j: (i, first + j, 0))],
        out_specs=pl.BlockSpec((1, w, blk), lambda i, j: (i, 0, j)),
        out_shape=jax.ShapeDtypeStruct((b, w, rows), x.dtype),
        compiler_params=pltpu.CompilerParams(dimension_semantics=("parallel", "parallel")),
        name="transpose_tail",
    )(x)


def _sample_attn_kernel(sink_ref, swa_ref, dil_ref, qa_ref, kva_ref, qb_ref, kvb_ref,
                        swa_out_ref, dil_out_ref, oa_ref, ob_ref, sa_scr, selfa_scr, sb_scr, selfb_scr):
    n = pl.program_id(0)
    win = swa_ref.shape[-1]
    span = dil_ref.shape[-1]
    scale = HEAD_DIM ** -0.5

    @pl.when(n == 0)
    def _():
        oa_ref[...] = jnp.zeros_like(oa_ref)
        ob_ref[...] = jnp.zeros_like(ob_ref)

    mine = lax.broadcasted_iota(jnp.int32, (1, qa_ref.shape[1]), 1) == n

    picked = {}

    def pick(ref, start):
        if (id(ref), start) not in picked:
            picked[(id(ref), start)] = jnp.sum(jnp.where(mine, ref[start:start + HEAD_DIM, :], 0.0), axis=1,
                                               keepdims=True)
        return picked[(id(ref), start)]

    def put(ref, start, col):
        ref[start:start + HEAD_DIM, :] = jnp.where(mine, col, ref[start:start + HEAD_DIM, :])

    def attend(s_scr, self_scr, k_of, v_of, q_of, knew_of, vnew_of, weight, self_weight, extra, o_ref):
        for h in range(H_A):
            q = q_of(h)
            s_scr[h:h + 1, :] = jnp.sum(k_of(h) * q, axis=0, keepdims=True)
            self_scr[h:h + 1, :] = jnp.broadcast_to(jnp.sum(knew_of(h) * q, axis=0, keepdims=True), (1, LANES))
        s = s_scr[...] * scale
        s_self = self_scr[:, 0:1] * scale
        if weight is not None:
            s = jnp.where(weight > 0.0, s, NEG)
        m = jnp.maximum(jnp.max(s, axis=1, keepdims=True), s_self)
        p = jnp.exp(s - m)
        if weight is not None:
            p = p * weight
        p_self = self_weight * jnp.exp(s_self - m)
        l = jnp.sum(p, axis=1, keepdims=True) + p_self
        if extra is not None:
            l = l + jnp.exp(extra - m)
        s_scr[...] = p / l
        self_scr[...] = jnp.broadcast_to(p_self / l, self_scr.shape)
        for h in range(H_A):
            o = jnp.sum(v_of(h) * s_scr[h:h + 1, :], axis=1, keepdims=True) + vnew_of(h) * self_scr[h:h + 1, 0:1]
            put(o_ref, h * HEAD_DIM, o)

    def shifted(x, new):
        width = x.shape[-1]
        rolled = pltpu.roll(x, width - 1, 1)
        last = lax.broadcasted_iota(jnp.int32, (1, width), 1) == width - 1
        return jnp.where(last, new, rolled)

    group = H_A // KV_A
    head_id = lax.broadcasted_iota(jnp.int32, (H_A, 1), 0)
    sink = jnp.zeros((H_A, 1), F32)
    for hq in range(H_A):
        sink = jnp.where(head_id == hq, sink_ref[hq], sink)
    attend(sa_scr, selfa_scr,
           lambda h: swa_ref[0, 0, h // group], lambda h: swa_ref[0, 1, h // group],
           lambda h: pick(qa_ref, h * HEAD_DIM),
           lambda h: pick(kva_ref, (h // group) * HEAD_DIM), lambda h: pick(kva_ref, K_A + (h // group) * HEAD_DIM),
           None, 1.0, sink, oa_ref)
    for kv in range(2):
        for h in range(KV_A):
            swa_out_ref[0, kv, h] = shifted(swa_ref[0, kv, h], pick(kva_ref, kv * K_A + h * HEAD_DIM))

    pos = lax.broadcasted_iota(jnp.int32, (1, span), 1)
    weight = jnp.zeros((1, span), F32)
    for d in DILATIONS:
        weight = weight + jnp.where((pos % d == 0) & (pos >= span - BLOCK * d), 1.0, 0.0)
    attend(sb_scr, selfb_scr,
           lambda h: dil_ref[0, 0, h], lambda h: dil_ref[0, 1, h], lambda h: pick(qb_ref, h * HEAD_DIM),
           lambda h: pick(kvb_ref, h * HEAD_DIM), lambda h: pick(kvb_ref, Q_B + h * HEAD_DIM),
           weight, float(len(DILATIONS)), None, ob_ref)
    for h in range(H_B):
        dil_out_ref[0, 0, h] = shifted(dil_ref[0, 0, h], pick(kvb_ref, h * HEAD_DIM))
        dil_out_ref[0, 1, h] = shifted(dil_ref[0, 1, h], pick(kvb_ref, Q_B + h * HEAD_DIM))


def _sample_attn(sinks, swa, dil, qa_t, kva_t, qb_t, kvb_t):
    n = swa.shape[0]
    assert swa.shape[-1] == BLOCK and dil.shape[-1] == CHUNK
    per_sample = lambda a: pl.BlockSpec((1,) + a.shape[1:], lambda i: (i, 0, 0, 0, 0))
    full = lambda a: pl.BlockSpec(a.shape, lambda i: (0, 0))
    o_shape = jax.ShapeDtypeStruct((Q_A, n), F32)
    return pl.pallas_call(
        _sample_attn_kernel,
        grid=(n,),
        in_specs=[pl.BlockSpec(memory_space=pltpu.SMEM), per_sample(swa), per_sample(dil),
                  full(qa_t), full(kva_t), full(qb_t), full(kvb_t)],
        out_specs=[per_sample(swa), per_sample(dil), full(qa_t), full(qb_t)],
        out_shape=[jax.ShapeDtypeStruct(swa.shape, F32), jax.ShapeDtypeStruct(dil.shape, F32), o_shape, o_shape],
        scratch_shapes=[pltpu.VMEM((H_A, BLOCK), F32), pltpu.VMEM((H_A, LANES), F32),
                        pltpu.VMEM((H_B, CHUNK), F32), pltpu.VMEM((H_B, LANES), F32)],
        compiler_params=pltpu.CompilerParams(dimension_semantics=("arbitrary",), vmem_limit_bytes=VMEM_LIMIT),
        name="sample_attn",
    )(sinks, swa, dil, qa_t, kva_t, qb_t, kvb_t)


def _mix_kernel(oa_ref, ob_ref, x_ref, ga_ref, gb_ref, wo_ref, gf_ref, wq_ref, keys_ref, xmid_ref, ht_ref, st_ref):
    o = jnp.concatenate([_rms(oa_ref[...], ga_ref[...]), _rms(ob_ref[...], gb_ref[...])], axis=-1)
    xm = x_ref[...] + jnp.dot(o.astype(BF16), wo_ref[...], preferred_element_type=F32)
    xmid_ref[...] = xm
    h = _rms(xm, gf_ref[...])
    hb = h.astype(BF16)
    ht_ref[...] = h.T.astype(BF16)
    q = jnp.dot(hb, wq_ref[...], preferred_element_type=F32).astype(BF16)
    half = PEER_QDIM // 2
    for i in range(2 * PEER_HEADS):
        st_ref[i] = lax.dot_general(keys_ref[i], q[:, i * half:(i + 1) * half], (((1,), (1,)), ((), ())),
                                    preferred_element_type=F32)


def _mix(oa, ob, x, ga, gb, wo, gf, wq, keys, tm):
    n = x.shape[0]
    nk = keys.shape[0]
    row = lambda width: pl.BlockSpec((tm, width), lambda i: (i, 0))
    full = lambda a: pl.BlockSpec(a.shape, lambda i: (0,) * a.ndim)
    return pl.pallas_call(
        _mix_kernel,
        grid=(n // tm,),
        in_specs=[row(Q_A), row(Q_B), row(D_MODEL), full(ga), full(gb), full(wo), full(gf), full(wq), full(keys)],
        out_specs=[row(D_MODEL), pl.BlockSpec((D_MODEL, tm), lambda i: (0, i)),
                   pl.BlockSpec((nk, N_KEYS, tm), lambda i: (0, 0, i))],
        out_shape=[jax.ShapeDtypeStruct((n, D_MODEL), F32), jax.ShapeDtypeStruct((D_MODEL, n), BF16),
                   jax.ShapeDtypeStruct((nk, N_KEYS, n), F32)],
        compiler_params=pltpu.CompilerParams(dimension_semantics=("parallel",), vmem_limit_bytes=VMEM_LIMIT),
        name="mix_out",
    )(oa, ob, x, ga, gb, wo, gf, wq, keys)


_CAND = [(i, j) for i in range(PEER_TOPK) for j in range(PEER_TOPK) if (i + 1) * (j + 1) <= PEER_TOPK]


def _merge_exchange_network(n):
    pairs = []
    p = 1
    while p < n:
        k = p
        while k >= 1:
            for j in range(k % p, n - k, 2 * k):
                for i in range(min(k, n - j - k)):
                    if (i + j) // (2 * p) == (i + j + k) // (2 * p):
                        pairs.append((i + j, i + j + k))
            k //= 2
        p *= 2
    return pairs


_SORT_NET = _merge_exchange_network(N_KEYS // SUBLANES)


def _topk_kernel(s_ref, g1_ref, g2_ref, thr_ref, a_scr, b_scr):
    def per_head(h, carry):
        for p, (g_ref, scr) in enumerate(((g1_ref, a_scr), (g2_ref, b_scr))):
            s = s_ref[2 * h + p]
            e = jnp.exp(s - jnp.max(s, axis=0, keepdims=True))
            g_ref[h] = e
            rows = [e[i * SUBLANES:(i + 1) * SUBLANES, :] for i in range(N_KEYS // SUBLANES)]
            for i, j in _SORT_NET:
                rows[i], rows[j] = jnp.maximum(rows[i], rows[j]), jnp.minimum(rows[i], rows[j])
            for k in range(PEER_TOPK):
                mk = jnp.max(rows[0], axis=0, keepdims=True)
                scr[k, pl.ds(h, 1), :] = mk
                popped = rows[0] == mk
                for d in range(PEER_TOPK - 1 - k):
                    rows[d] = jnp.where(popped, rows[d + 1], rows[d])
        return carry

    lax.fori_loop(0, PEER_HEADS, per_head, 0)

    a = [a_scr[k] for k in range(PEER_TOPK)]
    b = [b_scr[k] for k in range(PEER_TOPK)]
    cand = {(i, j): a[i] * b[j] for i, j in _CAND}
    lists = [[cand[(i, j)] for j in range(PEER_TOPK // (i + 1))] for i in range(PEER_TOPK)]
    z = jnp.zeros_like(a[0])
    theta = z
    for k in range(PEER_TOPK):
        mk = functools.reduce(jnp.maximum, [lst[0] for lst in lists])
        z = z + jnp.maximum(mk, 0.0)
        theta = jnp.where(mk > 0.0, mk, theta)
        for lst in lists:
            popped = lst[0] == mk
            keep = min(len(lst), PEER_TOPK - k)
            for d in range(keep):
                lst[d] = jnp.where(popped, lst[d + 1] if d + 1 < len(lst) else -1.0, lst[d])
    rz = 0.5 / z
    thr = jnp.full_like(z, 2.0)
    for (i, j), c in cand.items():
        thr = jnp.minimum(thr, jnp.where(c >= theta, (a[i] * rz) * b[j], 2.0))
    thr_ref[...] = thr
    for h in range(PEER_HEADS):
        g1_ref[h] = g1_ref[h] * rz[h:h + 1, :]


def _peer_topk(st, tb):
    nk, _, n = st.shape
    heads = nk // 2
    g_spec = pl.BlockSpec((heads, N_KEYS, tb), lambda i: (0, 0, i))
    g_shape = jax.ShapeDtypeStruct((heads, N_KEYS, n), F32)
    return pl.pallas_call(
        _topk_kernel,
        grid=(n // tb,),
        in_specs=[pl.BlockSpec((nk, N_KEYS, tb), lambda i: (0, 0, i))],
        out_specs=[g_spec, g_spec, pl.BlockSpec((heads, tb), lambda i: (0, i))],
        out_shape=[g_shape, g_shape, jax.ShapeDtypeStruct((heads, n), F32)],
        scratch_shapes=[pltpu.VMEM((PEER_TOPK, heads, tb), F32), pltpu.VMEM((PEER_TOPK, heads, tb), F32)],
        compiler_params=pltpu.CompilerParams(dimension_semantics=("parallel",), vmem_limit_bytes=VMEM_LIMIT),
        name="peer_topk",
    )(st)


def _transpose_cast_kernel(x_ref, o_ref):
    o_ref[...] = x_ref[...].T.astype(o_ref.dtype)


def _transpose_cast(x, dtype, blk):
    r, c = x.shape
    return pl.pallas_call(
        _transpose_cast_kernel,
        grid=(r // blk, c // blk),
        in_specs=[pl.BlockSpec((blk, blk), lambda i, j: (i, j))],
        out_specs=pl.BlockSpec((blk, blk), lambda i, j: (j, i)),
        out_shape=jax.ShapeDtypeStruct((c, r), dtype),
        compiler_params=pltpu.CompilerParams(dimension_semantics=("parallel", "parallel")),
        name="transpose_cast",
    )(x)


_GELU_C0 = 0.7978845608028654
_GELU_C1 = _GELU_C0 * 0.044715


def _peer_kernel(u_ref, vt_ref, ht_ref, g1_ref, g2_ref, thr_ref, xmid_ref, gfin_ref, y_ref,
                 acc, act0, act1, gate0, gate1):
    e = pl.program_id(1)
    n_i1 = u_ref.shape[0] // N_KEYS
    tt = act0.shape[1]
    nt = ht_ref.shape[1] // tt

    @pl.when(e == 0)
    def _():
        acc[...] = jnp.zeros_like(acc)

    def tile(t):
        return slice(t * tt, (t + 1) * tt)

    def act_matmul(buf, t):
        half = u_ref.shape[0] // 2
        for r in (slice(0, half), slice(half, 2 * half)):
            buf[r, :] = jnp.dot(u_ref[r, :], ht_ref[:, tile(t)], preferred_element_type=F32)

    def out_matmul(buf, t):
        acc[:, tile(t)] += jnp.dot(vt_ref[...], buf[...], preferred_element_type=F32)

    def gating(act_buf, gate_buf, t):
        for sub in range(tt // LANES):
            tok = slice(t * tt + sub * LANES, t * tt + (sub + 1) * LANES)
            cols = slice(sub * LANES, (sub + 1) * LANES)
            for il in range(n_i1):
                w = jnp.zeros((N_KEYS, LANES), F32)
                for h in range(PEER_HEADS):
                    prod = g1_ref[h, il:il + 1, tok] * g2_ref[h, :, tok]
                    w = w + jnp.where(prod >= thr_ref[h:h + 1, tok], prod, 0.0)
                rows = slice(il * N_KEYS, (il + 1) * N_KEYS)
                x = act_buf[rows, cols].astype(BF16)
                t_ = jnp.tanh(x * (_GELU_C0 + _GELU_C1 * (x * x)))
                gate_buf[rows, cols] = (w.astype(BF16) * x) * (1.0 + t_)

    acts, gates = (act0, act1), (gate0, gate1)
    act_matmul(acts[0], 0)
    for t in range(nt):
        if t + 1 < nt:
            act_matmul(acts[(t + 1) % 2], t + 1)
        gating(acts[t % 2], gates[t % 2], t)
        if t >= 1:
            out_matmul(gates[(t - 1) % 2], t - 1)
    out_matmul(gates[(nt - 1) % 2], nt - 1)

    @pl.when(e == pl.num_programs(1) - 1)
    def _():
        y_ref[...] = _rms(xmid_ref[...] + acc[...].T, gfin_ref[...])


def _peer(u_bf16, vt_bf16, ht, g1, g2, thr, xmid, gfin, tb, eb):
    n = xmid.shape[0]
    heads = g1.shape[0]
    tt = min(tb, 2 * LANES)
    return pl.pallas_call(
        _peer_kernel,
        grid=(n // tb, N_EXPERTS // eb),
        in_specs=[
            pl.BlockSpec((eb, D_MODEL), lambda t, e: (e, 0)),
            pl.BlockSpec((D_MODEL, eb), lambda t, e: (0, e)),
            pl.BlockSpec((D_MODEL, tb), lambda t, e: (0, t)),
            pl.BlockSpec((heads, eb // N_KEYS, tb), lambda t, e: (0, e, t)),
            pl.BlockSpec((heads, N_KEYS, tb), lambda t, e: (0, 0, t)),
            pl.BlockSpec((heads, tb), lambda t, e: (0, t)),
            pl.BlockSpec((tb, D_MODEL), lambda t, e: (t, 0)),
            pl.BlockSpec((1, D_MODEL), lambda t, e: (0, 0)),
        ],
        out_specs=pl.BlockSpec((tb, D_MODEL), lambda t, e: (t, 0)),
        out_shape=jax.ShapeDtypeStruct((n, D_MODEL), F32),
        scratch_shapes=[pltpu.VMEM((D_MODEL, tb), F32), pltpu.VMEM((eb, tt), F32), pltpu.VMEM((eb, tt), F32),
                        pltpu.VMEM((eb, tt), BF16), pltpu.VMEM((eb, tt), BF16)],
        compiler_params=pltpu.CompilerParams(
            dimension_semantics=("parallel", "arbitrary"), vmem_limit_bytes=VMEM_LIMIT),
        name="peer_dense",
    )(u_bf16, vt_bf16, ht, g1, g2, thr, xmid, gfin)


def _ffn_and_norm(oa, ob, x, params, tm, tb_topk, tb, eb):
    xmid, ht, st = _mix(oa, ob, x, params["ga"], params["gb"], params["wo"], params["gf"], params["wq"],
                        params["keys"], tm)
    g1, g2, thr = _peer_topk(st, tb_topk)
    return _peer(params["u"], params["vt"], ht, g1, g2, thr, xmid, params["gfin"], tb, eb)


def kernel(x_prompt, x_sample, cache_swa_kv, cache_dil_kv, attn_norm, w_in, sinks, out_norm_a, out_norm_b, w_out,
           ffn_norm, peer_w_q, peer_sub_keys, peer_u, peer_v, final_norm):
    depth = w_in.shape[0]
    assert depth == 1
    batch, seq, _ = x_prompt.shape
    n_s, s_len, _ = x_sample.shape
    assert s_len == 1 and seq % CHUNK == 0
    win_a = min(BLOCK, seq)
    win_b = min(CHUNK, seq)

    w_in_b = w_in[0].astype(BF16)
    g_attn = attn_norm[0][None]
    params = dict(
        ga=out_norm_a[0][None], gb=out_norm_b[0][None], wo=w_out[0].astype(BF16), gf=ffn_norm[0][None],
        wq=peer_w_q[0].astype(BF16),
        keys=peer_sub_keys[0].reshape(2 * PEER_HEADS, N_KEYS, PEER_QDIM // 2).astype(BF16),
        u=peer_u[0].astype(BF16), vt=_transpose_cast(peer_v[0], BF16, 1024), gfin=final_norm[None])
    sink_row = jnp.repeat(sinks[0].astype(F32), HEAD_DIM)[None]
    to_rows = lambda a: jnp.transpose(a, (0, 4, 1, 2, 3))[None]
    to_lanes = lambda a: jnp.transpose(a[0], (0, 2, 3, 4, 1))

    xp = x_prompt.reshape(batch * seq, D_MODEL)
    qa, kva, qb, kvb = _project(xp, jnp.arange(seq), g_attn, w_in_b, 512)
    oa, ob = _attn_prompt(qa, kva, qb, kvb, sink_row, batch, seq)
    y_prompt = _ffn_and_norm(oa, ob, xp, params, 256, 256, 1024, 1024).reshape(batch, seq, D_MODEL)
    swa_p = to_rows(_transpose_tail(kva.reshape(batch, seq, 2 * K_A), win_a).reshape(batch, 2, KV_A, HEAD_DIM, win_a))
    dil_p = to_rows(_transpose_tail(kvb.reshape(batch, seq, 2 * Q_B), win_b).reshape(batch, 2, H_B, HEAD_DIM, win_b))

    xs = x_sample.reshape(n_s, D_MODEL)
    pos_s = jnp.full((n_s,), PAST_LEN, jnp.int32)
    qa_s, kva_s, qb_s, kvb_s = _project(xs, pos_s, g_attn, w_in_b, n_s)
    swa_new, dil_new, oa_t, ob_t = _sample_attn(sinks[0].astype(F32), to_lanes(cache_swa_kv), to_lanes(cache_dil_kv),
                                                qa_s.T, kva_s.T, qb_s.T, kvb_s.T)
    y_sample = _ffn_and_norm(oa_t.T, ob_t.T, xs, params, n_s, 128, 128, 1024).reshape(n_s, 1, D_MODEL)
    return y_prompt, y_sample, swa_p, dil_p, to_rows(swa_new), to_rows(dil_new)
```
